```python
import functools
import jax
import jax.numpy as jnp
from jax import lax
import numpy as np

D_MODEL = 1024
BATCH = 4
SEQ = 8192
DEPTH = 4

GRID_W = 64
CTX_LEN = 256
N_MIXERS = 4
BLOCK = 128
ROPE_THETA = 10000.0
EPS = 1e-6
NEG_INF = -1e30
N_MOD = 6

A_HEADS = 8
A_KV_HEADS = 2
A_HEAD_DIM = 128

B_HEADS = 8
B_Q_LORA = 384
B_KV_LORA = 256
B_NOPE_DIM = 128
B_ROPE_DIM = 64
B_V_DIM = 128

C_WIDTH = 1024
C_GROUPS = 8
C_CHUNK = 128

D_HEADS = 16
D_KV_HEADS = 2
D_HEAD_DIM = 64
D_WINDOW = 128

FF_DENSE = 2816
N_EXPERTS = 8
TOP_K = 2
FF_EXPERT = 3584

kernel_name = 'hybrid_interleaved_diffusion_trunk'


def rmsnorm(t, g):
    tf = t.astype(jnp.float32)
    y = tf * lax.rsqrt(jnp.mean(tf * tf, axis=-1, keepdims=True) + EPS)
    return (y * g.astype(jnp.float32)).astype(t.dtype)


def layernorm(t, g, b):
    tf = t.astype(jnp.float32)
    mu = jnp.mean(tf, axis=-1, keepdims=True)
    var = jnp.mean(jnp.square(tf - mu), axis=-1, keepdims=True)
    y = (tf - mu) * lax.rsqrt(var + EPS)
    return (y * g.astype(jnp.float32) + b.astype(jnp.float32)).astype(t.dtype)


def modulate(t, g, shift, scale):
    return rmsnorm(t, g) * (1 + scale) + shift


def axial_rope_tables(rows, cols, dim):
    quarter = dim // 4
    inv_freq = ROPE_THETA ** (-jnp.arange(quarter, dtype=jnp.float32) / quarter)
    ang_r = rows.astype(jnp.float32)[:, None] * inv_freq
    ang_c = cols.astype(jnp.float32)[:, None] * inv_freq
    ang = jnp.concatenate([ang_r, ang_r, ang_c, ang_c], axis=-1)
    return jnp.cos(ang), jnp.sin(ang)


def apply_rope(t, cos, sin):
    shp = t.shape
    tr = t.reshape(shp[:-1] + (2, 2, shp[-1] // 4))
    rot = jnp.stack([-tr[..., 1, :], tr[..., 0, :]], axis=-2).reshape(shp)
    return t * cos[:, None, :].astype(t.dtype) + rot * sin[:, None, :].astype(t.dtype)


def softmax_with_sink(s, sink):
    m = jnp.maximum(jnp.max(s, axis=-1, keepdims=True), sink)
    e = jnp.exp(s - m)
    return e / (jnp.sum(e, axis=-1, keepdims=True) + jnp.exp(sink - m))


def dense_block_attention(q, k, v, scale, sink=None):
    B, L, Hk, G, dq = q.shape
    nb = L // BLOCK
    qb = jnp.moveaxis(q.reshape(B, nb, BLOCK, Hk, G, dq), 1, 0)

    def one_block(qi):
        s = jnp.einsum('bqhgd,bkhd->bhgqk', qi, k, preferred_element_type=jnp.float32) * scale
        if sink is None:
            p = jax.nn.softmax(s, axis=-1)
        else:
            p = softmax_with_sink(s, sink[None, :, :, None, None])
        return jnp.einsum('bhgqk,bkhd->bqhgd', p.astype(v.dtype), v)

    o = lax.map(one_block, qb)
    return jnp.moveaxis(o, 0, 1).reshape(B, L, Hk * G * v.shape[-1])


def window_sink_attention(q, k, v, kc, vc, sink, scale):
    B, S, Hk, G, dh = q.shape
    nb = S // BLOCK
    n_ctx = kc.shape[1]

    def neighbour_blocks(t):
        tb = jnp.pad(t.reshape(B, nb, BLOCK, Hk, t.shape[-1]), ((0, 0), (1, 1), (0, 0), (0, 0), (0, 0)))
        w = jnp.concatenate([tb[:, :-2], tb[:, 1:-1], tb[:, 2:]], axis=2)
        return jnp.moveaxis(w, 1, 0)

    kw, vw = neighbour_blocks(k), neighbour_blocks(v)
    qb = jnp.moveaxis(q.reshape(B, nb, BLOCK, Hk, G, dh), 1, 0)
    offs = jnp.arange(3 * BLOCK)
    rel = offs[None, :] - BLOCK - jnp.arange(BLOCK)[:, None]
    kblock = jnp.arange(nb)[:, None] + offs[None, :] // BLOCK - 1
    mask = (jnp.abs(rel) <= D_WINDOW)[None] & ((kblock >= 0) & (kblock < nb))[:, None, :]
    sink_b = sink[None, :, :, None, None]

    def one_block(args):
        qi, ki, vi, mi = args
        s_ctx = jnp.einsum('bqhgd,bkhd->bhgqk', qi, kc, preferred_element_type=jnp.float32) * scale
        s_win = jnp.einsum('bqhgd,bkhd->bhgqk', qi, ki, preferred_element_type=jnp.float32) * scale
        s = jnp.concatenate([s_ctx, jnp.where(mi, s_win, NEG_INF)], axis=-1)
        p = softmax_with_sink(s, sink_b).astype(v.dtype)
        return (jnp.einsum('bhgqk,bkhd->bqhgd', p[..., :n_ctx], vc)
                + jnp.einsum('bhgqk,bkhd->bqhgd', p[..., n_ctx:], vi))

    o = lax.map(one_block, (qb, kw, vw, mask))
    return jnp.moveaxis(o, 0, 1).reshape(B, S, Hk * G * v.shape[-1])


def gqa_project(t, wqkv, n_heads, n_kv, dh):
    B, L, _ = t.shape
    q, k, v = jnp.split(t @ wqkv, [n_heads * dh, (n_heads + n_kv) * dh], axis=-1)
    return q.reshape(B, L, n_heads, dh), k.reshape(B, L, n_kv, dh), v.reshape(B, L, n_kv, dh)


def mixer_gqa_axial(h, hc, rope, wqkv, q_norm, k_norm, wo, need_ctx):
    B, S, _ = h.shape
    C = hc.shape[1]
    G = A_HEADS // A_KV_HEADS
    scale = A_HEAD_DIM ** -0.5
    q, k, v = gqa_project(h, wqkv, A_HEADS, A_KV_HEADS, A_HEAD_DIM)
    qc, kc, vc = gqa_project(hc, wqkv, A_HEADS, A_KV_HEADS, A_HEAD_DIM)
    q = apply_rope(rmsnorm(q, q_norm), *rope)
    k = apply_rope(rmsnorm(k, k_norm), *rope)
    kc = rmsnorm(kc, k_norm)
    k_all = jnp.concatenate([kc, k], axis=1)
    v_all = jnp.concatenate([vc, v], axis=1)
    o = dense_block_attention(q.reshape(B, S, A_KV_HEADS, G, A_HEAD_DIM), k_all, v_all, scale) @ wo
    oc = None
    if need_ctx:
        qc = rmsnorm(qc, q_norm).reshape(B, C, A_KV_HEADS, G, A_HEAD_DIM)
        oc = dense_block_attention(qc, kc, vc, scale) @ wo
    return o, oc


def mla_project(t, rope, w_down, q_lora_norm, kv_lora_norm, w_uq, w_ukv):
    B, L, _ = t.shape
    dq, dkv, k_rope = jnp.split(t @ w_down, [B_Q_LORA, B_Q_LORA + B_KV_LORA], axis=-1)
    q = (rmsnorm(dq, q_lora_norm) @ w_uq).reshape(B, L, B_HEADS, B_NOPE_DIM + B_ROPE_DIM)
    kv = (rmsnorm(dkv, kv_lora_norm) @ w_ukv).reshape(B, L, B_HEADS, B_NOPE_DIM + B_V_DIM)
    q_nope, q_rope = jnp.split(q, [B_NOPE_DIM], axis=-1)
    k_nope, v = jnp.split(kv, [B_NOPE_DIM], axis=-1)
    k_rope = k_rope[:, :, None, :]
    if rope is not None:
        q_rope = apply_rope(q_rope, *rope)
        k_rope = apply_rope(k_rope, *rope)
    q = jnp.concatenate([q_nope, q_rope], axis=-1)[:, :, :, None, :]
    k = jnp.concatenate([k_nope, jnp.broadcast_to(k_rope, (B, L, B_HEADS, B_ROPE_DIM))], axis=-1)
    return q, k, v


def mixer_mla(h, hc, rope, w_down, q_lora_norm, kv_lora_norm, w_uq, w_ukv, wo, need_ctx):
    scale = (B_NOPE_DIM + B_ROPE_DIM) ** -0.5
    q, k, v = mla_project(h, rope, w_down, q_lora_norm, kv_lora_norm, w_uq, w_ukv)
    qc, kc, vc = mla_project(hc, None, w_down, q_lora_norm, kv_lora_norm, w_uq, w_ukv)
    k_all = jnp.concatenate([kc, k], axis=1)
    v_all = jnp.concatenate([vc, v], axis=1)
    o = dense_block_attention(q, k_all, v_all, scale) @ wo
    oc = dense_block_attention(qc, kc, vc, scale) @ wo if need_ctx else None
    return o, oc


def sgu_sequence(t, w_in, ln_g, ln_b, w_spatial, b_spatial, w_out):
    B, L, _ = t.shape
    u, v = jnp.split(jax.nn.gelu(t @ w_in, approximate=False), 2, axis=-1)
    v = layernorm(v, ln_g, ln_b).reshape(B, L // C_CHUNK, C_CHUNK, C_GROUPS, C_WIDTH // C_GROUPS)
    mixed = jnp.einsum('gpq,bnqgc->bnpgc', w_spatial, v) + b_spatial.T[:, :, None]
    return (u * mixed.reshape(B, L, C_WIDTH)) @ w_out


def mixer_swa_sink(h, hc, rope, wqkv, sinks, wo, need_ctx):
    B, S, _ = h.shape
    C = hc.shape[1]
    G = D_HEADS // D_KV_HEADS
    scale = D_HEAD_DIM ** -0.5
    q, k, v = gqa_project(h, wqkv, D_HEADS, D_KV_HEADS, D_HEAD_DIM)
    qc, kc, vc = gqa_project(hc, wqkv, D_HEADS, D_KV_HEADS, D_HEAD_DIM)
    q = apply_rope(q, *rope)
    k = apply_rope(k, *rope)
    sink = sinks.astype(jnp.float32).reshape(D_KV_HEADS, G)
    o = window_sink_attention(q.reshape(B, S, D_KV_HEADS, G, D_HEAD_DIM), k, v, kc, vc, sink, scale) @ wo
    oc = None
    if need_ctx:
        oc = dense_block_attention(qc.reshape(B, C, D_KV_HEADS, G, D_HEAD_DIM), kc, vc, scale, sink) @ wo
    return o, oc


def swiglu(t, w13, w2):
    a, b = jnp.split(t @ w13, 2, axis=-1)
    return (jax.nn.silu(a) * b) @ w2


def moe_swiglu(t, router, w13, w2):
    logits = jnp.einsum('bld,de->ble', t, router, preferred_element_type=jnp.float32)
    top_val, top_idx = lax.top_k(logits, TOP_K)
    weights = jax.nn.softmax(top_val, axis=-1)
    gates = jnp.einsum('blk,blke->ble', weights,
                       jax.nn.one_hot(top_idx, N_EXPERTS, dtype=jnp.float32)).astype(t.dtype)
    out = jnp.zeros_like(t)
    for e in range(N_EXPERTS):
        out = out + gates[..., e:e + 1] * swiglu(t, w13[e], w2[e])
    return out


def setup_inputs(seed: int = 0) -> dict:
    key = jax.random.key(seed)
    keys = iter(jax.random.split(key, 64))
    f32 = jnp.float32

    def normal(shape, std):
        return std * jax.random.normal(next(keys), shape, f32)

    def gain(shape):
        return 1.0 + 0.05 * jax.random.normal(next(keys), shape, f32)

    D = D_MODEL
    n_cyc = DEPTH // N_MIXERS
    n_pair = DEPTH // 2
    a_qkv = (A_HEADS + 2 * A_KV_HEADS) * A_HEAD_DIM
    d_qkv = (D_HEADS + 2 * D_KV_HEADS) * D_HEAD_DIM
    return {
        'x': normal((BATCH, SEQ, D), 1.0),
        'c': normal((BATCH, D), 1.0),
        'ctx': normal((BATCH, CTX_LEN, D), 1.0),
        'c_ctx': normal((D,), 1.0),
        'ada_w': normal((DEPTH, D, N_MOD * D), 0.5 * D ** -0.5),
        'ada_b': normal((DEPTH, N_MOD * D), 0.02),
        'norm_mix': gain((DEPTH, D)),
        'norm_ffn': gain((DEPTH, D)),
        'final_norm': gain((D,)),
        'a_wqkv': normal((n_cyc, D, a_qkv), D ** -0.5),
        'a_q_norm': gain((n_cyc, A_HEAD_DIM)),
        'a_k_norm': gain((n_cyc, A_HEAD_DIM)),
        'a_wo': normal((n_cyc, A_HEADS * A_HEAD_DIM, D), (A_HEADS * A_HEAD_DIM) ** -0.5),
        'b_w_down': normal((n_cyc, D, B_Q_LORA + B_KV_LORA + B_ROPE_DIM), D ** -0.5),
        'b_q_lora_norm': gain((n_cyc, B_Q_LORA)),
        'b_kv_lora_norm': gain((n_cyc, B_KV_LORA)),
        'b_w_uq': normal((n_cyc, B_Q_LORA, B_HEADS * (B_NOPE_DIM + B_ROPE_DIM)), B_Q_LORA ** -0.5),
        'b_w_ukv': normal((n_cyc, B_KV_LORA, B_HEADS * (B_NOPE_DIM + B_V_DIM)), B_KV_LORA ** -0.5),
        'b_wo': normal((n_cyc, B_HEADS * B_V_DIM, D), (B_HEADS * B_V_DIM) ** -0.5),
        'c_w_in': normal((n_cyc, D, 2 * C_WIDTH), D ** -0.5),
        'c_ln_g': gain((n_cyc, C_WIDTH)),
        'c_ln_b': normal((n_cyc, C_WIDTH), 0.02),
        'c_w_spatial': normal((n_cyc, C_GROUPS, C_CHUNK, C_CHUNK), C_CHUNK ** -0.5),
        'c_b_spatial': 1.0 + normal((n_cyc, C_GROUPS, C_CHUNK), 0.1),
        'c_w_out': normal((n_cyc, C_WIDTH, D), C_WIDTH ** -0.5),
        'd_wqkv': normal((n_cyc, D, d_qkv), D ** -0.5),
        'd_sinks': normal((n_cyc, D_HEADS), 1.0),
        'd_wo': normal((n_cyc, D_HEADS * D_HEAD_DIM, D), (D_HEADS * D_HEAD_DIM) ** -0.5),
        'ffn_w13': normal((n_pair, D, 2 * FF_DENSE), D ** -0.5),
        'ffn_w2': normal((n_pair, FF_DENSE, D), FF_DENSE ** -0.5),
        'moe_router': normal((n_pair, D, N_EXPERTS), D ** -0.5),
        'moe_w13': normal((n_pair, N_EXPERTS, D, 2 * FF_EXPERT), D ** -0.5),
        'moe_w2': normal((n_pair, N_EXPERTS, FF_EXPERT, D), FF_EXPERT ** -0.5),
    }


def reference(x, c, ctx, c_ctx, ada_w, ada_b, norm_mix, norm_ffn, final_norm,
              a_wqkv, a_q_norm, a_k_norm, a_wo,
              b_w_down, b_q_lora_norm, b_kv_lora_norm, b_w_uq, b_w_ukv, b_wo,
              c_w_in, c_ln_g, c_ln_b, c_w_spatial, c_b_spatial, c_w_out,
              d_wqkv, d_sinks, d_wo,
              ffn_w13, ffn_w2, moe_router, moe_w13, moe_w2):
    B, S, _ = x.shape
    n_rows = S // GRID_W
    rows = jnp.repeat(jnp.arange(n_rows, dtype=jnp.int32), GRID_W)
    cols = jnp.tile(jnp.arange(GRID_W, dtype=jnp.int32), n_rows)
    ropes = {d: axial_rope_tables(rows, cols, d) for d in sorted({A_HEAD_DIM, B_ROPE_DIM, D_HEAD_DIM})}

    mod_lat = (jnp.einsum('bd,ldm->lbm', jax.nn.silu(c), ada_w) + ada_b[:, None, :])[:, :, None, :]
    mod_ctx = (jnp.einsum('d,ldm->lm', jax.nn.silu(c_ctx), ada_w) + ada_b)[:, None, None, :]

    h, hc = x, ctx
    for i in range(DEPTH):
        j, kind = i // N_MIXERS, i % N_MIXERS
        need_ctx = i < DEPTH - 1
        sh1, sc1, g1, sh2, sc2, g2 = jnp.split(mod_lat[i], N_MOD, axis=-1)
        csh1, csc1, cg1, csh2, csc2, cg2 = jnp.split(mod_ctx[i], N_MOD, axis=-1)
        a = modulate(h, norm_mix[i], sh1, sc1)
        ac = modulate(hc, norm_mix[i], csh1, csc1)
        if kind == 0:
            o, oc = mixer_gqa_axial(a, ac, ropes[A_HEAD_DIM], a_wqkv[j], a_q_norm[j], a_k_norm[j], a_wo[j], need_ctx)
        elif kind == 1:
            o, oc = mixer_mla(a, ac, ropes[B_ROPE_DIM], b_w_down[j], b_q_lora_norm[j], b_kv_lora_norm[j],
                              b_w_uq[j], b_w_ukv[j], b_wo[j], need_ctx)
        elif kind == 2:
            sgu = functools.partial(sgu_sequence, w_in=c_w_in[j], ln_g=c_ln_g[j], ln_b=c_ln_b[j],
                                    w_spatial=c_w_spatial[j], b_spatial=c_b_spatial[j], w_out=c_w_out[j])
            o = sgu(a)
            oc = sgu(ac) if need_ctx else None
        else:
            o, oc = mixer_swa_sink(a, ac, ropes[D_HEAD_DIM], d_wqkv[j], d_sinks[j], d_wo[j], need_ctx)
        h = h + g1 * o

        p = i // 2
        if i % 2 == 0:
            ffn = functools.partial(swiglu, w13=ffn_w13[p], w2=ffn_w2[p])
        else:
            ffn = functools.partial(moe_swiglu, router=moe_router[p], w13=moe_w13[p], w2=moe_w2[p])
        h = h + g2 * ffn(modulate(h, norm_ffn[i], sh2, sc2))
        if need_ctx:
            hc = hc + cg1 * oc
            hc = hc + cg2 * ffn(modulate(hc, norm_ffn[i], csh2, csc2))
    return rmsnorm(h, final_norm)
```

```python
import functools
import math

import jax
import jax.numpy as jnp
from jax import lax
from jax.experimental import pallas as pl
from jax.experimental.pallas import tpu as pltpu

F32 = jnp.float32
BF16 = jnp.bfloat16

EPS = 1e-6
ROPE_THETA = 10000.0
GRID_W = 64
N_MOD = 6
MASK_VALUE = -1e30

A_HEADS, A_KV_HEADS, A_HEAD_DIM = 8, 2, 128
B_HEADS, B_Q_LORA, B_KV_LORA, B_NOPE_DIM, B_ROPE_DIM, B_V_DIM = 8, 384, 256, 128, 64, 128
C_GROUPS, C_CHUNK = 8, 128
D_HEADS, D_KV_HEADS, D_HEAD_DIM, D_WINDOW = 16, 2, 64, 128
TOP_K = 2

SEG = 256
LANES = 128
MXU_N = 256
VMEM_LIMIT = 56 * 1024 * 1024

MOD_ROWS = 8


def _cparams(n_axes):
    return pltpu.CompilerParams(dimension_semantics=("arbitrary",) * n_axes,
                                vmem_limit_bytes=VMEM_LIMIT)


def _mod_row(tile, s, segs_per_tile, segs_per_sample, n_batch):
    g = tile * segs_per_tile + s
    b = lax.div(g, jnp.int32(segs_per_sample))
    r = g - b * segs_per_sample
    return jnp.where(r == 0, n_batch, b)


def _mod_vec(mod_ref, row, k, d):
    return mod_ref[pl.ds(row, 1), k * d:(k + 1) * d]


def _rms(x):
    return x * lax.rsqrt(jnp.mean(x * x, axis=-1, keepdims=True) + EPS)


def _modulate_tile(x_ref, xn_ref, mod_ref, g_ref, tile, k_shift, *, segs_per_tile, segs_per_sample, n_batch):
    d = x_ref.shape[-1]
    g = g_ref[...]
    for s in range(segs_per_tile):
        row = _mod_row(tile, s, segs_per_tile, segs_per_sample, n_batch)
        shift = _mod_vec(mod_ref, row, k_shift, d)
        scale = _mod_vec(mod_ref, row, k_shift + 1, d)
        x = x_ref[s * SEG:(s + 1) * SEG, :]
        y = _rms(x) * g * (1.0 + scale) + shift
        xn_ref[s * SEG:(s + 1) * SEG, :] = y.astype(xn_ref.dtype)


def _rope(t, cos, sin_a, sin_b, shift):
    return (t * cos + pltpu.roll(t, LANES - shift, axis=1) * sin_a
            + pltpu.roll(t, shift, axis=1) * sin_b)


def _ada_kernel(c_ref, w_ref, b_ref, o_ref):
    c = c_ref[...]
    a = c * jax.nn.sigmoid(c)
    o_ref[0] = jnp.dot(a, w_ref[0], preferred_element_type=F32,
                       precision=lax.Precision.HIGHEST) + b_ref[0]


def _ada_mod(cvec, ada_w, ada_b):
    depth, d, m = ada_w.shape
    tn = m // 4
    return pl.pallas_call(
        _ada_kernel,
        grid=(depth, m // tn),
        in_specs=[pl.BlockSpec((MOD_ROWS, d), lambda l, j: (0, 0)),
                  pl.BlockSpec((1, d, tn), lambda l, j: (l, 0, j)),
                  pl.BlockSpec((1, 1, tn), lambda l, j: (l, 0, j))],
        out_specs=pl.BlockSpec((1, MOD_ROWS, tn), lambda l, j: (l, 0, j)),
        out_shape=jax.ShapeDtypeStruct((depth, MOD_ROWS, m), F32),
        compiler_params=_cparams(2),
        name="ada_mod",
    )(cvec, ada_w, ada_b.reshape(depth, 1, m))


def _proj_a_kernel(x_ref, mod_ref, g_ref, w_ref, qn_ref, kn_ref, cos_ref, sa_ref, sb_ref, o_ref, xn_ref,
                   *, geom, scale):
    i = pl.program_id(0)
    _modulate_tile(x_ref, xn_ref, mod_ref, g_ref, i, 0, **geom)
    xn = xn_ref[...]
    cos, sa, sb = cos_ref[...], sa_ref[...], sb_ref[...]
    qn, kn = qn_ref[...], kn_ref[...]
    n_q, n_k = A_HEADS, A_KV_HEADS
    n_heads = n_q + 2 * n_k
    for pair in range(n_heads // 2):
        y2 = jnp.dot(xn, w_ref[:, pair * MXU_N:(pair + 1) * MXU_N], preferred_element_type=F32)
        for half in range(2):
            h = 2 * pair + half
            y = y2[:, half * LANES:(half + 1) * LANES]
            if h < n_q:
                y = _rope(_rms(y) * qn, cos, sa, sb, A_HEAD_DIM // 4) * scale
            elif h < n_q + n_k:
                y = _rope(_rms(y) * kn, cos, sa, sb, A_HEAD_DIM // 4)
            o_ref[:, h * LANES:(h + 1) * LANES] = y.astype(o_ref.dtype)


def _proj_a(x, mod, g, w, qn, kn, tabs, geom, tm):
    t, d = x.shape
    n = w.shape[1]
    cos, sa, sb = tabs
    full = lambda shape: pl.BlockSpec(shape, lambda i: (0,) * len(shape))
    rows = lambda width: pl.BlockSpec((tm, width), lambda i: (i, 0))
    return pl.pallas_call(
        functools.partial(_proj_a_kernel, geom=geom, scale=A_HEAD_DIM ** -0.5),
        grid=(t // tm,),
        in_specs=[rows(d), full(mod.shape), full((1, d)), full(w.shape), full((1, LANES)), full((1, LANES)),
                  rows(LANES), rows(LANES), rows(LANES)],
        out_specs=rows(n),
        out_shape=jax.ShapeDtypeStruct((t, n), BF16),
        scratch_shapes=[pltpu.VMEM((tm, d), BF16)],
        compiler_params=_cparams(1),
        name="proj_a",
    )(x, mod, g, w, qn, kn, cos, sa, sb)


def _flash_kernel(q_ref, k_ref, v_ref, o_ref, *, ctx, tk, n_lat_chunks, ctx_q_tiles):
    qi = pl.program_id(2)
    q = q_ref[...]
    tq = q.shape[0]
    dv = v_ref.shape[-1]

    def step(k, v, m, l, acc):
        s = lax.dot_general(q, k, (((1,), (1,)), ((), ())), preferred_element_type=F32)
        m_new = jnp.maximum(m, jnp.max(s, axis=-1, keepdims=True))
        alpha = jnp.exp(m - m_new)
        p = jnp.exp(s - m_new)
        l = alpha * l + jnp.sum(p, axis=-1, keepdims=True)
        acc = alpha * acc + jnp.dot(p.astype(v.dtype), v, preferred_element_type=F32)
        return m_new, l, acc

    m0 = jnp.full((tq, 1), MASK_VALUE, F32)
    l0 = jnp.zeros((tq, 1), F32)
    acc0 = jnp.zeros((tq, dv), F32)
    carry = step(k_ref[0:ctx, :], v_ref[0:ctx, :], m0, l0, acc0)

    def body(c, carry):
        off = pl.multiple_of(ctx + c * tk, SEG)
        return step(k_ref[pl.ds(off, tk), :], v_ref[pl.ds(off, tk), :], *carry)

    n = jnp.where(qi < ctx_q_tiles, 0, n_lat_chunks)
    m, l, acc = lax.fori_loop(0, n, body, carry)
    o_ref[...] = (acc / l).astype(o_ref.dtype)


def _flash(q, k, v, *, n_batch, seq, ctx, n_heads, dq, dv, q_col, k_col, v_col, tq=256, tk=512):
    t = q.shape[0]
    nq = seq // tq
    kern = functools.partial(_flash_kernel, ctx=ctx, tk=tk, n_lat_chunks=(seq - ctx) // tk, ctx_q_tiles=ctx // tq)
    return pl.pallas_call(
        kern,
        grid=(n_batch, n_heads, nq),
        in_specs=[pl.BlockSpec((tq, dq), lambda b, h, i: (b * nq + i, q_col(h))),
                  pl.BlockSpec((seq, dq), lambda b, h, i: (b, k_col(h))),
                  pl.BlockSpec((seq, dv), lambda b, h, i: (b, v_col(h)))],
        out_specs=pl.BlockSpec((tq, dv), lambda b, h, i: (b * nq + i, h)),
        out_shape=jax.ShapeDtypeStruct((t, n_heads * dv), BF16),
        compiler_params=_cparams(3),
        name="flash_attn",
    )(q, k, v)


def _proj_res_kernel(a_ref, w_ref, h_ref, mod_ref, o_ref, *, geom, k_gate):
    i = pl.program_id(0)
    d = h_ref.shape[-1]
    for s in range(geom["segs_per_tile"]):
        row = _mod_row(i, s, **geom)
        gate = _mod_vec(mod_ref, row, k_gate, d)
        sl = slice(s * SEG, (s + 1) * SEG)
        y = jnp.dot(a_ref[sl, :], w_ref[...], preferred_element_type=F32)
        o_ref[sl, :] = h_ref[sl, :] + gate * y


def _proj_res(a, w, h, mod, geom, tm, k_gate=2):
    t, d = h.shape
    ka = a.shape[1]
    return pl.pallas_call(
        functools.partial(_proj_res_kernel, geom=geom, k_gate=k_gate),
        grid=(t // tm,),
        in_specs=[pl.BlockSpec((tm, ka), lambda i: (i, 0)),
                  pl.BlockSpec(w.shape, lambda i: (0, 0)),
                  pl.BlockSpec((tm, d), lambda i: (i, 0)),
                  pl.BlockSpec(mod.shape, lambda i: (0, 0))],
        out_specs=pl.BlockSpec((tm, d), lambda i: (i, 0)),
        out_shape=jax.ShapeDtypeStruct((t, d), F32),
        compiler_params=_cparams(1),
        name="proj_res",
    )(a, w, h, mod)


def _ffn_kernel(x_ref, mod_ref, g_ref, gates_ref, w13_ref, w2_ref, o_ref, xn_ref, acc_ref, *, geom, n_experts):
    i, e, f = pl.program_id(0), pl.program_id(1), pl.program_id(2)
    d = x_ref.shape[-1]
    tf = w2_ref.shape[0]

    @pl.when((e == 0) & (f == 0))
    def _():
        _modulate_tile(x_ref, xn_ref, mod_ref, g_ref, i, 3, **geom)
        acc_ref[...] = jnp.zeros_like(acc_ref)

    ab = jnp.dot(xn_ref[...], w13_ref[...], preferred_element_type=F32)
    a, b = ab[:, :tf], ab[:, tf:]
    y = a * jax.nn.sigmoid(a) * b
    if n_experts > 1:
        lane = lax.broadcasted_iota(jnp.int32, gates_ref.shape, 1)
        gate = jnp.sum(jnp.where(lane == e, gates_ref[...], 0.0), axis=1, keepdims=True)
        y = y * gate
    acc_ref[...] += jnp.dot(y.astype(BF16), w2_ref[...], preferred_element_type=F32)

    @pl.when((e == n_experts - 1) & (f == pl.num_programs(2) - 1))
    def _():
        for s in range(geom["segs_per_tile"]):
            row = _mod_row(i, s, **geom)
            gate2 = _mod_vec(mod_ref, row, 5, d)
            sl = slice(s * SEG, (s + 1) * SEG)
            o_ref[sl, :] = x_ref[sl, :] + gate2 * acc_ref[sl, :]


def _ffn(x, mod, g, gates, w13, w2, geom, tm):
    t, d = x.shape
    n_experts, nf, _, tf2 = w13.shape
    tf = tf2 // 2
    return pl.pallas_call(
        functools.partial(_ffn_kernel, geom=geom, n_experts=n_experts),
        grid=(t // tm, n_experts, nf),
        in_specs=[pl.BlockSpec((tm, d), lambda i, e, f: (i, 0)),
                  pl.BlockSpec(mod.shape, lambda i, e, f: (0, 0)),
                  pl.BlockSpec((1, d), lambda i, e, f: (0, 0)),
                  pl.BlockSpec((tm, gates.shape[1]), lambda i, e, f: (i, 0)),
                  pl.BlockSpec((None, None, d, tf2), lambda i, e, f: (e, f, 0, 0)),
                  pl.BlockSpec((None, None, tf, d), lambda i, e, f: (e, f, 0, 0))],
        out_specs=pl.BlockSpec((tm, d), lambda i, e, f: (i, 0)),
        out_shape=jax.ShapeDtypeStruct((t, d), F32),
        scratch_shapes=[pltpu.VMEM((tm, d), BF16), pltpu.VMEM((tm, d), F32)],
        compiler_params=_cparams(3),
        name="ffn",
    )(x, mod, g, gates, w13, w2)


def _prep_w13(w13, tf):
    *lead, d, f2 = w13.shape
    f = f2 // 2
    w = w13.reshape(*lead, d, 2, f // tf, tf)
    w = jnp.moveaxis(w, -2, -4)
    return w.reshape(*lead, f // tf, d, 2 * tf).astype(BF16)


def _prep_w2(w2, tf):
    *lead, f, d = w2.shape
    return w2.reshape(*lead, f // tf, tf, d).astype(BF16)


def _router_kernel(x_ref, mod_ref, g_ref, wr_ref, o_ref, xn_ref, *, geom):
    i = pl.program_id(0)
    _modulate_tile(x_ref, xn_ref, mod_ref, g_ref, i, 3, **geom)
    logits = jnp.dot(xn_ref[...], wr_ref[...], preferred_element_type=F32, precision=lax.Precision.HIGHEST)
    n_e = float(logits.shape[1])
    lane = lax.broadcasted_iota(jnp.int32, logits.shape, 1).astype(F32)
    m1 = jnp.max(logits, axis=1, keepdims=True)
    i1 = jnp.min(jnp.where(logits == m1, lane, n_e), axis=1, keepdims=True)
    rest = jnp.where(lane == i1, -jnp.inf, logits)
    m2 = jnp.max(rest, axis=1, keepdims=True)
    i2 = jnp.min(jnp.where(rest == m2, lane, n_e), axis=1, keepdims=True)
    e2 = jnp.exp(m2 - m1)
    w1 = 1.0 / (1.0 + e2)
    w2 = e2 / (1.0 + e2)
    o_ref[...] = jnp.where(lane == i1, w1, 0.0) + jnp.where(lane == i2, w2, 0.0)


def _router(x, mod, g, wr, geom, tm):
    t, d = x.shape
    n_e = wr.shape[1]
    return pl.pallas_call(
        functools.partial(_router_kernel, geom=geom),
        grid=(t // tm,),
        in_specs=[pl.BlockSpec((tm, d), lambda i: (i, 0)),
                  pl.BlockSpec(mod.shape, lambda i: (0, 0)),
                  pl.BlockSpec((1, d), lambda i: (0, 0)),
                  pl.BlockSpec(wr.shape, lambda i: (0, 0))],
        out_specs=pl.BlockSpec((tm, n_e), lambda i: (i, 0)),
        out_shape=jax.ShapeDtypeStruct((t, n_e), F32),
        scratch_shapes=[pltpu.VMEM((tm, d), F32)],
        compiler_params=_cparams(1),
        name="router",
    )(x, mod, g, wr)


def _proj_b_kernel(x_ref, mod_ref, g_ref, wd_ref, qg_ref, kvg_ref, wq_ref, wk_ref, wv_ref,
                   cos_ref, sa_ref, sb_ref, q_ref, k_ref, v_ref, xn_ref, *, geom, scale):
    i = pl.program_id(0)
    _modulate_tile(x_ref, xn_ref, mod_ref, g_ref, i, 0, **geom)
    xn = xn_ref[...]
    cos, sa, sb = cos_ref[...], sa_ref[...], sb_ref[...]
    dq = jnp.dot(xn, wd_ref[:, :B_Q_LORA], preferred_element_type=F32)
    dkv = jnp.dot(xn, wd_ref[:, B_Q_LORA:B_Q_LORA + B_KV_LORA], preferred_element_type=F32)
    kr = jnp.dot(xn, wd_ref[:, B_Q_LORA + B_KV_LORA:], preferred_element_type=F32)
    dqn = (_rms(dq) * qg_ref[...]).astype(BF16)
    dkvn = (_rms(dkv) * kvg_ref[...]).astype(BF16)
    kr = _rope(kr, cos, sa, sb, B_ROPE_DIM // 4).astype(k_ref.dtype)
    for h in range(B_HEADS):
        qh = jnp.dot(dqn, wq_ref[:, h * MXU_N:(h + 1) * MXU_N], preferred_element_type=F32)
        q_ref[:, h * MXU_N:h * MXU_N + LANES] = (qh[:, :LANES] * scale).astype(q_ref.dtype)
        q_ref[:, h * MXU_N + LANES:(h + 1) * MXU_N] = (
            _rope(qh[:, LANES:], cos, sa, sb, B_ROPE_DIM // 4) * scale).astype(q_ref.dtype)
        k_ref[:, h * MXU_N + LANES:(h + 1) * MXU_N] = kr
    for pair in range(B_HEADS // 2):
        sl = slice(pair * MXU_N, (pair + 1) * MXU_N)
        kn = jnp.dot(dkvn, wk_ref[:, sl], preferred_element_type=F32)
        for half in range(2):
            h = 2 * pair + half
            k_ref[:, h * MXU_N:h * MXU_N + LANES] = kn[:, half * LANES:(half + 1) * LANES].astype(k_ref.dtype)
        v_ref[:, sl] = jnp.dot(dkvn, wv_ref[:, sl], preferred_element_type=F32).astype(v_ref.dtype)


def _proj_b(x, mod, g, wd, qg, kvg, wq, wk, wv, tabs, geom, tm):
    t, d = x.shape
    cos, sa, sb = tabs
    full = lambda a: pl.BlockSpec(a.shape, lambda i: (0,) * a.ndim)
    rows = lambda width: pl.BlockSpec((tm, width), lambda i: (i, 0))
    nq, nv = B_HEADS * MXU_N, B_HEADS * B_V_DIM
    return pl.pallas_call(
        functools.partial(_proj_b_kernel, geom=geom, scale=(B_NOPE_DIM + B_ROPE_DIM) ** -0.5),
        grid=(t // tm,),
        in_specs=[rows(d), full(mod), full(g), full(wd), full(qg), full(kvg), full(wq), full(wk), full(wv),
                  rows(LANES), rows(LANES), rows(LANES)],
        out_specs=[rows(nq), rows(nq), rows(nv)],
        out_shape=[jax.ShapeDtypeStruct((t, nq), BF16), jax.ShapeDtypeStruct((t, nq), BF16),
                   jax.ShapeDtypeStruct((t, nv), BF16)],
        scratch_shapes=[pltpu.VMEM((tm, d), BF16)],
        compiler_params=_cparams(1),
        name="proj_b",
    )(x, mod, g, wd, qg, kvg, wq, wk, wv, cos, sa, sb)


def _gelu(x):
    return 0.5 * x * (1.0 + lax.erf(x * (2.0 ** -0.5)))


def _sgu_kernel(x_ref, mod_ref, g_ref, win_ref, lng_ref, lnb_ref, ws_ref, bs_ref, wout_ref, o_ref,
                xn_ref, u_ref, v_ref, gated_ref, *, geom):
    i = pl.program_id(0)
    tm, d = x_ref.shape
    cw = u_ref.shape[1]
    _modulate_tile(x_ref, xn_ref, mod_ref, g_ref, i, 0, **geom)
    xn = xn_ref[...]
    for j in range(cw // MXU_N):
        sl = slice(j * MXU_N, (j + 1) * MXU_N)
        u_ref[:, sl] = _gelu(jnp.dot(xn, win_ref[:, sl], preferred_element_type=F32))
        v_ref[:, sl] = _gelu(jnp.dot(xn, win_ref[:, cw + j * MXU_N:cw + (j + 1) * MXU_N],
                                     preferred_element_type=F32))
    lng, lnb = lng_ref[...], lnb_ref[...]
    gw = cw // C_GROUPS
    for c in range(tm // C_CHUNK):
        rs = slice(c * C_CHUNK, (c + 1) * C_CHUNK)
        v = v_ref[rs, :]
        mu = jnp.mean(v, axis=-1, keepdims=True)
        vc = v - mu
        var = jnp.mean(vc * vc, axis=-1, keepdims=True)
        vn = (vc * lax.rsqrt(var + EPS) * lng + lnb).astype(BF16)
        for gi in range(C_GROUPS):
            cs = slice(gi * gw, (gi + 1) * gw)
            mixed = jnp.dot(ws_ref[gi], vn[:, cs], preferred_element_type=F32) + bs_ref[:, cs]
            gated_ref[rs, cs] = (u_ref[rs, cs] * mixed).astype(gated_ref.dtype)
    for s in range(geom["segs_per_tile"]):
        row = _mod_row(i, s, **geom)
        gate = _mod_vec(mod_ref, row, 2, d)
        sl = slice(s * SEG, (s + 1) * SEG)
        y = jnp.dot(gated_ref[sl, :], wout_ref[...], preferred_element_type=F32)
        o_ref[sl, :] = x_ref[sl, :] + gate * y


def _sgu(x, mod, g, win, lng, lnb, ws, bs, wout, geom, tm):
    t, d = x.shape
    cw = wout.shape[0]
    full = lambda a: pl.BlockSpec(a.shape, lambda i: (0,) * a.ndim)
    return pl.pallas_call(
        functools.partial(_sgu_kernel, geom=geom),
        grid=(t // tm,),
        in_specs=[pl.BlockSpec((tm, d), lambda i: (i, 0)), full(mod), full(g), full(win), full(lng), full(lnb),
                  full(ws), full(bs), full(wout)],
        out_specs=pl.BlockSpec((tm, d), lambda i: (i, 0)),
        out_shape=jax.ShapeDtypeStruct((t, d), F32),
        scratch_shapes=[pltpu.VMEM((tm, d), BF16), pltpu.VMEM((tm, cw), F32), pltpu.VMEM((tm, cw), F32),
                        pltpu.VMEM((tm, cw), BF16)],
        compiler_params=_cparams(1),
        name="sgu",
    )(x, mod, g, win, lng, lnb, ws, bs, wout)


def _proj_d_kernel(x_ref, mod_ref, g_ref, w_ref, cos_ref, sa_ref, sb_ref, q_ref, k_ref, v_ref, xn_ref,
                   *, geom, scale):
    i = pl.program_id(0)
    _modulate_tile(x_ref, xn_ref, mod_ref, g_ref, i, 0, **geom)
    xn = xn_ref[...]
    cos, sa, sb = cos_ref[...], sa_ref[...], sb_ref[...]
    nq = q_ref.shape[1]
    for pair in range(nq // MXU_N):
        y2 = jnp.dot(xn, w_ref[:, pair * MXU_N:(pair + 1) * MXU_N], preferred_element_type=F32)
        for half in range(2):
            sl = slice(half * LANES, (half + 1) * LANES)
            y = _rope(y2[:, sl], cos, sa, sb, D_HEAD_DIM // 4) * scale
            q_ref[:, pair * MXU_N + half * LANES:pair * MXU_N + (half + 1) * LANES] = y.astype(q_ref.dtype)
    kv = jnp.dot(xn, w_ref[:, nq:nq + MXU_N], preferred_element_type=F32)
    k_ref[...] = _rope(kv[:, :LANES], cos, sa, sb, D_HEAD_DIM // 4).astype(k_ref.dtype)
    v_ref[...] = kv[:, LANES:].astype(v_ref.dtype)


def _proj_d(x, mod, g, w, tabs, geom, tm):
    t, d = x.shape
    cos, sa, sb = tabs
    nq = D_HEADS * D_HEAD_DIM
    full = lambda a: pl.BlockSpec(a.shape, lambda i: (0,) * a.ndim)
    rows = lambda width: pl.BlockSpec((tm, width), lambda i: (i, 0))
    return pl.pallas_call(
        functools.partial(_proj_d_kernel, geom=geom, scale=D_HEAD_DIM ** -0.5),
        grid=(t // tm,),
        in_specs=[rows(d), full(mod), full(g), full(w), rows(LANES), rows(LANES), rows(LANES)],
        out_specs=[rows(nq), rows(LANES), rows(LANES)],
        out_shape=[jax.ShapeDtypeStruct((t, nq), BF16), jax.ShapeDtypeStruct((t, LANES), BF16),
                   jax.ShapeDtypeStruct((t, LANES), BF16)],
        scratch_shapes=[pltpu.VMEM((tm, d), BF16)],
        compiler_params=_cparams(1),
        name="proj_d",
    )(x, mod, g, w, cos, sa, sb)


def _win_kernel(sink_ref, q_ref, kc_ref, km_ref, k0_ref, kp_ref, vc_ref, vm_ref, v0_ref, vp_ref, o_ref,
                *, ctx_blocks, n_blocks):
    j = pl.program_id(1)
    blk = q_ref.shape[0]
    ctx = kc_ref.shape[0]
    kk = jnp.concatenate([kc_ref[...], km_ref[...], k0_ref[...], kp_ref[...]], axis=0)
    vv = jnp.concatenate([vc_ref[...], vm_ref[...], v0_ref[...], vp_ref[...]], axis=0)
    n_keys = kk.shape[0]
    qpos = lax.broadcasted_iota(jnp.int32, (blk, n_keys), 0)
    col = lax.broadcasted_iota(jnp.int32, (blk, n_keys), 1)
    rel = col - ctx - blk - qpos
    kblock = j - 1 + lax.shift_right_arithmetic(col - ctx, int(math.log2(blk)))
    far = jnp.where(j >= ctx_blocks, 0, 2 * D_WINDOW + 2)
    in_window = (jnp.abs(rel) + far <= D_WINDOW) & (kblock >= ctx_blocks) & (kblock < n_blocks)
    mask = (col < ctx) | in_window
    group = D_HEADS // D_KV_HEADS
    for h in range(D_HEADS):
        hk = h // group
        q = q_ref[:, h * D_HEAD_DIM:(h + 1) * D_HEAD_DIM]
        k = kk[:, hk * D_HEAD_DIM:(hk + 1) * D_HEAD_DIM]
        v = vv[:, hk * D_HEAD_DIM:(hk + 1) * D_HEAD_DIM]
        s = lax.dot_general(q, k, (((1,), (1,)), ((), ())), preferred_element_type=F32)
        s = jnp.where(mask, s, MASK_VALUE)
        sink = sink_ref[h]
        m = jnp.maximum(jnp.max(s, axis=-1, keepdims=True), sink)
        e = jnp.exp(s - m)
        denom = jnp.sum(e, axis=-1, keepdims=True) + jnp.exp(sink - m)
        o = jnp.dot(e.astype(v.dtype), v, preferred_element_type=F32) / denom
        o_ref[:, h * D_HEAD_DIM:(h + 1) * D_HEAD_DIM] = o.astype(o_ref.dtype)


def _win_attn(sinks, q, k, v, *, n_batch, seq, ctx, blk=128):
    t, nq = q.shape
    nb = seq // blk
    cb = ctx // blk
    lat = lambda j: jnp.clip(j, cb, nb - 1)
    kv_specs = [pl.BlockSpec((ctx, LANES), lambda b, j: (b * (seq // ctx), 0)),
                pl.BlockSpec((blk, LANES), lambda b, j: (b * nb + lat(j - 1), 0)),
                pl.BlockSpec((blk, LANES), lambda b, j: (b * nb + lat(j), 0)),
                pl.BlockSpec((blk, LANES), lambda b, j: (b * nb + lat(j + 1), 0))]
    return pl.pallas_call(
        functools.partial(_win_kernel, ctx_blocks=cb, n_blocks=nb),
        grid=(n_batch, nb),
        in_specs=[pl.BlockSpec(memory_space=pltpu.SMEM),
                  pl.BlockSpec((blk, nq), lambda b, j: (b * nb + j, 0))] + kv_specs + kv_specs,
        out_specs=pl.BlockSpec((blk, nq), lambda b, j: (b * nb + j, 0)),
        out_shape=jax.ShapeDtypeStruct((t, nq), BF16),
        compiler_params=_cparams(2),
        name="win_attn",
    )(sinks, q, k, k, k, k, v, v, v, v)


def _final_kernel(x_ref, g_ref, o_ref):
    o_ref[...] = _rms(x_ref[...]) * g_ref[...]


def _final_norm(x, g, *, n_batch, seq, ctx):
    t, d = x.shape
    lat = seq - ctx
    per = seq // SEG
    return pl.pallas_call(
        _final_kernel,
        grid=(n_batch, lat // SEG),
        in_specs=[pl.BlockSpec((SEG, d), lambda b, i: (b * per + ctx // SEG + i, 0)),
                  pl.BlockSpec((1, d), lambda b, i: (0, 0))],
        out_specs=pl.BlockSpec((None, SEG, d), lambda b, i: (b, i, 0)),
        out_shape=jax.ShapeDtypeStruct((n_batch, lat, d), F32),
        compiler_params=_cparams(2),
        name="final_norm",
    )(x, g)


def _rope_tables(n_batch, lat, ctx, dim, pad_to):
    quarter = dim // 4
    pos = jnp.arange(lat, dtype=jnp.int32)
    rows, cols = pos // GRID_W, pos % GRID_W
    inv_freq = ROPE_THETA ** (-jnp.arange(quarter, dtype=F32) / quarter)
    ang_r = rows.astype(F32)[:, None] * inv_freq
    ang_c = cols.astype(F32)[:, None] * inv_freq
    ang = jnp.concatenate([ang_r, ang_r, ang_c, ang_c], axis=-1)
    cos, sin = jnp.cos(ang), jnp.sin(ang)
    first = (jnp.arange(dim) % (2 * quarter)) < quarter
    sin_a = jnp.where(first, -sin, 0.0)
    sin_b = jnp.where(first, 0.0, sin)

    def widen(tab, fill):
        if pad_to == dim:
            return tab
        if pad_to % dim == 0 and fill is None:
            return jnp.tile(tab, (1, pad_to // dim))
        return jnp.concatenate([tab, jnp.full((lat, pad_to - dim), fill, F32)], axis=1)

    def unify(tab, fill):
        tab = jnp.concatenate([jnp.full((ctx, pad_to), fill, F32), tab], axis=0)
        return jnp.tile(tab, (n_batch, 1))

    return cos, sin_a, sin_b, widen, unify


def _tables_tiled(n_batch, lat, ctx, dim):
    cos, sa, sb, widen, unify = _rope_tables(n_batch, lat, ctx, dim, LANES)
    return unify(widen(cos, None), 1.0), unify(widen(sa, None), 0.0), unify(widen(sb, None), 0.0)


def _tables_padded(n_batch, lat, ctx, dim):
    cos, sa, sb, widen, unify = _rope_tables(n_batch, lat, ctx, dim, LANES)
    return unify(widen(cos, 1.0), 1.0), unify(widen(sa, 0.0), 0.0), unify(widen(sb, 0.0), 0.0)


def _pick_tf(f, cap):
    best = LANES
    for tf in range(LANES, cap + 1, LANES):
        if f % tf == 0:
            best = tf
    return best


def kernel(x, c, ctx, c_ctx, ada_w, ada_b, norm_mix, norm_ffn, final_norm, a_wqkv, a_q_norm, a_k_norm, a_wo, b_w_down, b_q_lora_norm, b_kv_lora_norm, b_w_uq, b_w_ukv, b_wo, c_w_in, c_ln_g, c_ln_b, c_w_spatial, c_b_spatial, c_w_out, d_wqkv, d_sinks, d_wo, ffn_w13, ffn_w2, moe_router, moe_w13, moe_w2):
    n_batch, lat, d = x.shape
    n_ctx = ctx.shape[1]
    depth = ada_w.shape[0]
    assert n_ctx == SEG and lat % 512 == 0 and n_batch < MOD_ROWS and depth == 4
    seq = n_ctx + lat
    t = n_batch * seq
    segs_total = t // SEG
    segs_per_tile = max(k for k in (4, 3, 2, 1) if segs_total % k == 0)
    tm = segs_per_tile * SEG
    geom = dict(segs_per_tile=segs_per_tile, segs_per_sample=seq // SEG, n_batch=n_batch)
    sgu_segs = 2 if segs_total % 2 == 0 else 1
    geom_sgu = dict(geom, segs_per_tile=sgu_segs)

    cvec = jnp.concatenate([c, c_ctx[None, :], jnp.zeros((MOD_ROWS - n_batch - 1, d), F32)], axis=0)
    mod = _ada_mod(cvec, ada_w, ada_b)
    h = jnp.concatenate([ctx, x], axis=1).reshape(t, d)

    tabs_a = _tables_tiled(n_batch, lat, n_ctx, A_HEAD_DIM)
    tabs_d = _tables_tiled(n_batch, lat, n_ctx, D_HEAD_DIM)
    tabs_b = _tables_padded(n_batch, lat, n_ctx, B_ROPE_DIM)
    row = lambda v: v.reshape(1, -1)
    dense_tf = _pick_tf(ffn_w13.shape[-1] // 2, 256)
    moe_tf = _pick_tf(moe_w13.shape[-1] // 2, 512)
    ones_gate = jnp.ones((t, 1), F32)

    qkv = _proj_a(h, mod[0], row(norm_mix[0]), a_wqkv[0].astype(BF16), row(a_q_norm[0]), row(a_k_norm[0]),
                  tabs_a, geom, tm)
    grp = A_HEADS // A_KV_HEADS
    o = _flash(qkv, qkv, qkv, n_batch=n_batch, seq=seq, ctx=n_ctx, n_heads=A_HEADS, dq=A_HEAD_DIM, dv=A_HEAD_DIM,
               q_col=lambda hh: hh, k_col=lambda hh: A_HEADS + hh // grp,
               v_col=lambda hh: A_HEADS + A_KV_HEADS + hh // grp)
    h = _proj_res(o, a_wo[0].astype(BF16), h, mod[0], geom, tm)
    h = _ffn(h, mod[0], row(norm_ffn[0]), ones_gate, _prep_w13(ffn_w13[0], dense_tf)[None],
             _prep_w2(ffn_w2[0], dense_tf)[None], geom, tm)

    wd = b_w_down[0]
    wd = jnp.concatenate([wd, jnp.zeros((d, LANES - B_ROPE_DIM), F32)], axis=1).astype(BF16)
    wq = b_w_uq[0].reshape(B_Q_LORA, B_HEADS, B_NOPE_DIM + B_ROPE_DIM)
    wq = jnp.concatenate([wq, jnp.zeros((B_Q_LORA, B_HEADS, MXU_N - B_NOPE_DIM - B_ROPE_DIM), F32)], axis=2)
    wq = wq.reshape(B_Q_LORA, B_HEADS * MXU_N).astype(BF16)
    wkv = b_w_ukv[0].reshape(B_KV_LORA, B_HEADS, B_NOPE_DIM + B_V_DIM)
    wk = wkv[:, :, :B_NOPE_DIM].reshape(B_KV_LORA, B_HEADS * B_NOPE_DIM).astype(BF16)
    wv = wkv[:, :, B_NOPE_DIM:].reshape(B_KV_LORA, B_HEADS * B_V_DIM).astype(BF16)
    qc, kc, vb = _proj_b(h, mod[1], row(norm_mix[1]), wd, row(b_q_lora_norm[0]), row(b_kv_lora_norm[0]),
                         wq, wk, wv, tabs_b, geom, tm)
    o = _flash(qc, kc, vb, n_batch=n_batch, seq=seq, ctx=n_ctx, n_heads=B_HEADS, dq=MXU_N, dv=B_V_DIM,
               q_col=lambda hh: hh, k_col=lambda hh: hh, v_col=lambda hh: hh)
    h = _proj_res(o, b_wo[0].astype(BF16), h, mod[1], geom, tm)
    gates = _router(h, mod[1], row(norm_ffn[1]), moe_router[0], geom, tm)
    h = _ffn(h, mod[1], row(norm_ffn[1]), gates, _prep_w13(moe_w13[0], moe_tf), _prep_w2(moe_w2[0], moe_tf),
             geom, tm)

    bs = jnp.repeat(c_b_spatial[0].T, c_w_out.shape[1] // C_GROUPS, axis=1)
    h = _sgu(h, mod[2], row(norm_mix[2]), c_w_in[0].astype(BF16), row(c_ln_g[0]), row(c_ln_b[0]),
             c_w_spatial[0].astype(BF16), bs, c_w_out[0].astype(BF16), geom_sgu, sgu_segs * SEG)
    h = _ffn(h, mod[2], row(norm_ffn[2]), ones_gate, _prep_w13(ffn_w13[1], dense_tf)[None],
             _prep_w2(ffn_w2[1], dense_tf)[None], geom, tm)

    q3, k3, v3 = _proj_d(h, mod[3], row(norm_mix[3]), d_wqkv[0].astype(BF16), tabs_d, geom, tm)
    o = _win_attn(d_sinks[0], q3, k3, v3, n_batch=n_batch, seq=seq, ctx=n_ctx)
    h = _proj_res(o, d_wo[0].astype(BF16), h, mod[3], geom, tm)
    gates = _router(h, mod[3], row(norm_ffn[3]), moe_router[1], geom, tm)
    h = _ffn(h, mod[3], row(norm_ffn[3]), gates, _prep_w13(moe_w13[1], moe_tf), _prep_w2(moe_w2[1], moe_tf),
             geom, tm)

    return _final_norm(h, row(final_norm), n_batch=n_batch, seq=seq, ctx=n_ctx)
```

```python
import functools
import math

import jax
import jax.numpy as jnp
from jax import lax
from jax.experimental import pallas as pl
from jax.experimental.pallas import tpu as pltpu

F32 = jnp.float32
BF16 = jnp.bfloat16

EPS = 1e-6
ROPE_THETA = 10000.0
GRID_W = 64
MASK_VALUE = -1e30
LOG2E = math.log2(math.e)

A_HEADS, A_KV_HEADS, A_HEAD_DIM = 8, 2, 128
B_HEADS, B_Q_LORA, B_KV_LORA, B_NOPE_DIM, B_ROPE_DIM, B_V_DIM = 8, 384, 256, 128, 64, 128
C_GROUPS, C_CHUNK = 8, 128
D_HEADS, D_KV_HEADS, D_HEAD_DIM, D_WINDOW = 16, 2, 64, 128
TOP_K = 2

SEG = 256
LANES = 128
MXU_N = 256
VMEM_LIMIT = 56 * 1024 * 1024

MOD_ROWS = 8
INFO_COLS = 8


def _cparams(n_axes):
    return pltpu.CompilerParams(dimension_semantics=("arbitrary",) * n_axes,
                                vmem_limit_bytes=VMEM_LIMIT)


def _full(a):
    return pl.BlockSpec(a.shape, lambda *_: (0,) * a.ndim)


def _mod_row(tile, s, segs_per_tile, lat_segs, segs_per_sample, n_batch):
    g = tile * segs_per_tile + s
    return jnp.where(g < lat_segs, lax.div(g, jnp.int32(segs_per_sample)), n_batch)


def _mod_vec(mod_ref, row, k, d):
    return mod_ref[pl.ds(row, 1), k * d:(k + 1) * d]


def _rms(x):
    return x * lax.rsqrt(jnp.mean(x * x, axis=-1, keepdims=True) + EPS)


def _modulate_tile(x_ref, xn_ref, mod_ref, g_ref, tile, k_shift, geom):
    d = x_ref.shape[-1]
    g = g_ref[...]
    for s in range(geom["segs_per_tile"]):
        row = _mod_row(tile, s, **geom)
        shift = _mod_vec(mod_ref, row, k_shift, d)
        scale = _mod_vec(mod_ref, row, k_shift + 1, d)
        x = x_ref[s * SEG:(s + 1) * SEG, :]
        y = _rms(x) * g * (1.0 + scale) + shift
        xn_ref[s * SEG:(s + 1) * SEG, :] = y.astype(xn_ref.dtype)


def _rope(t, cos, sin_a, sin_b, shift):
    return (t * cos + pltpu.roll(t, LANES - shift, axis=1) * sin_a
            + pltpu.roll(t, shift, axis=1) * sin_b)


def _ada_kernel(c_ref, w_ref, b_ref, o_ref):
    c = c_ref[...]
    a = c * jax.nn.sigmoid(c)
    o_ref[0] = jnp.dot(a, w_ref[0], preferred_element_type=F32,
                       precision=lax.Precision.HIGHEST) + b_ref[0]


def _ada_mod(cvec, ada_w, ada_b):
    depth, d, m = ada_w.shape
    tn = m // 4
    return pl.pallas_call(
        _ada_kernel,
        grid=(depth, m // tn),
        in_specs=[pl.BlockSpec((MOD_ROWS, d), lambda l, j: (0, 0)),
                  pl.BlockSpec((1, d, tn), lambda l, j: (l, 0, j)),
                  pl.BlockSpec((1, 1, tn), lambda l, j: (l, 0, j))],
        out_specs=pl.BlockSpec((1, MOD_ROWS, tn), lambda l, j: (l, 0, j)),
        out_shape=jax.ShapeDtypeStruct((depth, MOD_ROWS, m), F32),
        compiler_params=_cparams(2),
        name="ada_mod",
    )(cvec, ada_w, ada_b.reshape(depth, 1, m))


def _proj_a_kernel(x_ref, mod_ref, g_ref, w_ref, qn_ref, kn_ref, cos_ref, sa_ref, sb_ref, o_ref, xn_ref,
                   *, geom, scale):
    i = pl.program_id(0)
    _modulate_tile(x_ref, xn_ref, mod_ref, g_ref, i, 0, geom)
    xn = xn_ref[...]
    cos, sa, sb = cos_ref[...], sa_ref[...], sb_ref[...]
    qn, kn = qn_ref[...], kn_ref[...]
    n_q, n_k = A_HEADS, A_KV_HEADS
    n_heads = n_q + 2 * n_k
    for pair in range(n_heads // 2):
        y2 = jnp.dot(xn, w_ref[:, pair * MXU_N:(pair + 1) * MXU_N], preferred_element_type=F32)
        for half in range(2):
            h = 2 * pair + half
            y = y2[:, half * LANES:(half + 1) * LANES]
            if h < n_q:
                y = _rope(_rms(y) * qn, cos, sa, sb, A_HEAD_DIM // 4) * scale
            elif h < n_q + n_k:
                y = _rope(_rms(y) * kn, cos, sa, sb, A_HEAD_DIM // 4)
            o_ref[:, h * LANES:(h + 1) * LANES] = y.astype(o_ref.dtype)


def _proj_a(x, mod, g, w, qn, kn, tabs, geom, tm):
    t, d = x.shape
    n = w.shape[1]
    rows = lambda width: pl.BlockSpec((tm, width), lambda i: (i, 0))
    return pl.pallas_call(
        functools.partial(_proj_a_kernel, geom=geom, scale=A_HEAD_DIM ** -0.5 * LOG2E),
        grid=(t // tm,),
        in_specs=[rows(d), _full(mod), _full(g), _full(w), _full(qn), _full(kn),
                  rows(LANES), rows(LANES), rows(LANES)],
        out_specs=rows(n),
        out_shape=jax.ShapeDtypeStruct((t, n), BF16),
        scratch_shapes=[pltpu.VMEM((tm, d), BF16)],
        compiler_params=_cparams(1),
        name="proj_a",
    )(x, mod, g, w, qn, kn, *tabs)


def _flash_kernel(*refs, heads, dq, dv, tk, n_lat_chunks):
    if n_lat_chunks:
        _, q_ref, kc_ref, vc_ref, kl_ref, vl_ref, o_ref = refs
    else:
        _, q_ref, kc_ref, vc_ref, o_ref = refs
    tq = q_ref.shape[0]
    if heads > 1:
        q = jnp.concatenate([q_ref[:, g * dq:(g + 1) * dq] for g in range(heads)], axis=0)
    else:
        q = q_ref[...]
    rows = heads * tq

    def step(k, v, m, l, acc):
        s = lax.dot_general(q, k, (((1,), (1,)), ((), ())), preferred_element_type=F32)
        m_new = jnp.maximum(m, jnp.max(s, axis=-1, keepdims=True))
        alpha = jnp.exp2(m - m_new)
        p = jnp.exp2(s - m_new)
        l = alpha * l + jnp.sum(p, axis=-1, keepdims=True)
        acc = alpha * acc + jnp.dot(p.astype(v.dtype), v, preferred_element_type=F32)
        return m_new, l, acc

    m0 = jnp.full((rows, 1), MASK_VALUE, F32)
    l0 = jnp.zeros((rows, 1), F32)
    acc0 = jnp.zeros((rows, dv), F32)
    carry = step(kc_ref[...], vc_ref[...], m0, l0, acc0)
    if n_lat_chunks:
        def body(c, carry):
            off = pl.multiple_of(c * tk, tk)
            return step(kl_ref[pl.ds(off, tk), :], vl_ref[pl.ds(off, tk), :], *carry)
        carry = lax.fori_loop(0, n_lat_chunks, body, carry, unroll=4)
    m, l, acc = carry
    o = acc / l
    for g in range(heads):
        o_ref[:, g * dv:(g + 1) * dv] = o[g * tq:(g + 1) * tq, :].astype(o_ref.dtype)


def _flash(q, k, v, *, n_batch, lat, ctx, n_groups, heads, dq, dv, k_col, v_col, tq, tk=512):
    t = q.shape[0]
    nq = lat // tq
    ctx_blk0 = n_batch * lat // ctx
    kern = functools.partial(_flash_kernel, heads=heads, dq=dq, dv=dv, tk=tk)
    out_shape = jax.ShapeDtypeStruct((t, n_groups * heads * dv), BF16)
    o = pl.pallas_call(
        functools.partial(kern, n_lat_chunks=lat // tk),
        grid=(n_batch, n_groups, nq),
        in_specs=[pl.BlockSpec(memory_space=pl.ANY),
                  pl.BlockSpec((tq, heads * dq), lambda b, g, i: (b * nq + i, g)),
                  pl.BlockSpec((ctx, dq), lambda b, g, i: (ctx_blk0 + b, k_col(g))),
                  pl.BlockSpec((ctx, dv), lambda b, g, i: (ctx_blk0 + b, v_col(g))),
                  pl.BlockSpec((lat, dq), lambda b, g, i: (b, k_col(g))),
                  pl.BlockSpec((lat, dv), lambda b, g, i: (b, v_col(g)))],
        out_specs=pl.BlockSpec((tq, heads * dv), lambda b, g, i: (b * nq + i, g)),
        out_shape=out_shape,
        input_output_aliases={0: 0},
        compiler_params=_cparams(3),
        name="flash_lat",
    )(jnp.zeros(out_shape.shape, out_shape.dtype), q, k, v, k, v)
    return pl.pallas_call(
        functools.partial(kern, n_lat_chunks=0),
        grid=(n_batch, n_groups),
        in_specs=[pl.BlockSpec(memory_space=pl.ANY),
                  pl.BlockSpec((ctx, heads * dq), lambda b, g: (ctx_blk0 + b, g)),
                  pl.BlockSpec((ctx, dq), lambda b, g: (ctx_blk0 + b, k_col(g))),
                  pl.BlockSpec((ctx, dv), lambda b, g: (ctx_blk0 + b, v_col(g)))],
        out_specs=pl.BlockSpec((ctx, heads * dv), lambda b, g: (ctx_blk0 + b, g)),
        out_shape=out_shape,
        input_output_aliases={0: 0},
        compiler_params=_cparams(2),
        name="flash_ctx",
    )(o, q, k, v)


def _proj_res_kernel(a_ref, w_ref, h_ref, mod_ref, o_ref, *, geom, k_gate):
    i = pl.program_id(0)
    d = h_ref.shape[-1]
    for s in range(geom["segs_per_tile"]):
        row = _mod_row(i, s, **geom)
        gate = _mod_vec(mod_ref, row, k_gate, d)
        sl = slice(s * SEG, (s + 1) * SEG)
        y = jnp.dot(a_ref[sl, :], w_ref[...], preferred_element_type=F32)
        o_ref[sl, :] = h_ref[sl, :] + gate * y


def _proj_res(a, w, h, mod, geom, tm, k_gate=2):
    t, d = h.shape
    ka = a.shape[1]
    return pl.pallas_call(
        functools.partial(_proj_res_kernel, geom=geom, k_gate=k_gate),
        grid=(t // tm,),
        in_specs=[pl.BlockSpec((tm, ka), lambda i: (i, 0)), _full(w),
                  pl.BlockSpec((tm, d), lambda i: (i, 0)), _full(mod)],
        out_specs=pl.BlockSpec((tm, d), lambda i: (i, 0)),
        out_shape=jax.ShapeDtypeStruct((t, d), F32),
        compiler_params=_cparams(1),
        name="proj_res",
    )(a, w, h, mod)


def _swiglu_step(xn, w13_ref, w2_ref):
    tf = w2_ref.shape[0]
    ab = jnp.dot(xn, w13_ref[...], preferred_element_type=F32)
    a, b = ab[:, :tf], ab[:, tf:]
    y = a * jax.nn.sigmoid(a) * b
    return jnp.dot(y.astype(BF16), w2_ref[...], preferred_element_type=F32)


def _ffn_kernel(x_ref, mod_ref, g_ref, w13_ref, w2_ref, o_ref, xn_ref, acc_ref, *, geom):
    i, f = pl.program_id(0), pl.program_id(1)
    d = x_ref.shape[-1]

    @pl.when(f == 0)
    def _():
        _modulate_tile(x_ref, xn_ref, mod_ref, g_ref, i, 3, geom)
        acc_ref[...] = jnp.zeros_like(acc_ref)

    acc_ref[...] += _swiglu_step(xn_ref[...], w13_ref, w2_ref)

    @pl.when(f == pl.num_programs(1) - 1)
    def _():
        for s in range(geom["segs_per_tile"]):
            row = _mod_row(i, s, **geom)
            gate2 = _mod_vec(mod_ref, row, 5, d)
            sl = slice(s * SEG, (s + 1) * SEG)
            o_ref[sl, :] = x_ref[sl, :] + gate2 * acc_ref[sl, :]


def _ffn(x, mod, g, w13, w2, geom, tm):
    t, d = x.shape
    nf, _, tf2 = w13.shape
    tf = tf2 // 2
    return pl.pallas_call(
        functools.partial(_ffn_kernel, geom=geom),
        grid=(t // tm, nf),
        in_specs=[pl.BlockSpec((tm, d), lambda i, f: (i, 0)), _full(mod), _full(g),
                  pl.BlockSpec((None, d, tf2), lambda i, f: (f, 0, 0)),
                  pl.BlockSpec((None, tf, d), lambda i, f: (f, 0, 0))],
        out_specs=pl.BlockSpec((tm, d), lambda i, f: (i, 0)),
        out_shape=jax.ShapeDtypeStruct((t, d), F32),
        scratch_shapes=[pltpu.VMEM((tm, d), BF16), pltpu.VMEM((tm, d), F32)],
        compiler_params=_cparams(2),
        name="ffn",
    )(x, mod, g, w13, w2)


def _prep_w13(w13, tf):
    *lead, d, f2 = w13.shape
    f = f2 // 2
    w = w13.reshape(*lead, d, 2, f // tf, tf)
    w = jnp.moveaxis(w, -2, -4)
    return w.reshape(*lead, f // tf, d, 2 * tf).astype(BF16)


def _prep_w2(w2, tf):
    *lead, f, d = w2.shape
    return w2.reshape(*lead, f // tf, tf, d).astype(BF16)


def _router_kernel(x_ref, mod_ref, g_ref, wr_ref, info_ref, cnt_ref, xn_ref, tri_ref, run_ref, *, geom):
    i = pl.program_id(0)
    tm = x_ref.shape[0]

    @pl.when(i == 0)
    def _():
        r = lax.broadcasted_iota(jnp.int32, (tm, tm), 0)
        c = lax.broadcasted_iota(jnp.int32, (tm, tm), 1)
        tri_ref[...] = jnp.where(c < r, 1.0, 0.0).astype(tri_ref.dtype)
        run_ref[...] = jnp.zeros_like(run_ref)

    _modulate_tile(x_ref, xn_ref, mod_ref, g_ref, i, 3, geom)
    logits = jnp.dot(xn_ref[...], wr_ref[...], preferred_element_type=F32, precision=lax.Precision.HIGHEST)
    n_e = float(logits.shape[1])
    lane = lax.broadcasted_iota(jnp.int32, logits.shape, 1).astype(F32)
    m1 = jnp.max(logits, axis=1, keepdims=True)
    i1 = jnp.min(jnp.where(logits == m1, lane, n_e), axis=1, keepdims=True)
    rest = jnp.where(lane == i1, -jnp.inf, logits)
    m2 = jnp.max(rest, axis=1, keepdims=True)
    i2 = jnp.min(jnp.where(rest == m2, lane, n_e), axis=1, keepdims=True)
    e2 = jnp.exp(m2 - m1)
    w1 = 1.0 / (1.0 + e2)
    w2 = e2 / (1.0 + e2)
    onehot = jnp.where((lane == i1) | (lane == i2), 1.0, 0.0)
    before = jnp.dot(tri_ref[...], onehot.astype(tri_ref.dtype), preferred_element_type=F32) + run_ref[...]
    r1 = jnp.sum(jnp.where(lane == i1, before, 0.0), axis=1, keepdims=True)
    r2 = jnp.sum(jnp.where(lane == i2, before, 0.0), axis=1, keepdims=True)
    run_ref[...] += jnp.sum(onehot, axis=0, keepdims=True)
    cnt_ref[...] = run_ref[...]
    vals = (i1, i2, w1, w2, r1, r2)
    info = jnp.zeros_like(logits)
    for k, val in enumerate(vals):
        info = jnp.where(lane == float(k), val, info)
    info_ref[...] = info


def _router(x, mod, g, wr, geom, tm):
    t, d = x.shape
    n_e = wr.shape[1]
    assert n_e == INFO_COLS
    return pl.pallas_call(
        functools.partial(_router_kernel, geom=geom),
        grid=(t // tm,),
        in_specs=[pl.BlockSpec((tm, d), lambda i: (i, 0)), _full(mod), _full(g), _full(wr)],
        out_specs=[pl.BlockSpec((tm, INFO_COLS), lambda i: (i, 0)),
                   pl.BlockSpec((1, n_e), lambda i: (0, 0))],
        out_shape=[jax.ShapeDtypeStruct((t, INFO_COLS), F32), jax.ShapeDtypeStruct((1, n_e), F32)],
        scratch_shapes=[pltpu.VMEM((tm, d), F32), pltpu.VMEM((tm, tm), BF16), pltpu.VMEM((1, n_e), F32)],
        compiler_params=_cparams(1),
        name="router",
    )(x, mod, g, wr)


def _row_copy(src_ref, src_row, dst_ref, dst_row, sem):
    return pltpu.make_async_copy(src_ref.at[pl.ds(src_row, 1)], dst_ref.at[pl.ds(dst_row, 1)], sem)


def _dispatch_kernel(slots_ref, x_ref, mod_ref, g_ref, xs_in_ref, xs_ref, xn_ref, sem, *, geom):
    del xs_in_ref
    i = pl.program_id(0)
    tm = x_ref.shape[0]
    _modulate_tile(x_ref, xn_ref, mod_ref, g_ref, i, 3, geom)

    def body(r, carry):
        for k in range(TOP_K):
            _row_copy(xn_ref, r, xs_ref, slots_ref[TOP_K * r + k], sem).start()
        return carry

    lax.fori_loop(0, tm, body, 0)
    for k in range(TOP_K):
        pltpu.make_async_copy(xn_ref, xs_ref.at[pl.ds(0, tm)], sem).wait()


def _dispatch(slots, x, mod, g, xs_zero, geom, tm):
    t, d = x.shape
    return pl.pallas_call(
        functools.partial(_dispatch_kernel, geom=geom),
        grid=(t // tm,),
        in_specs=[pl.BlockSpec((TOP_K * tm,), lambda i: (i,), memory_space=pltpu.SMEM),
                  pl.BlockSpec((tm, d), lambda i: (i, 0)), _full(mod), _full(g),
                  pl.BlockSpec(memory_space=pl.ANY)],
        out_specs=pl.BlockSpec(memory_space=pl.ANY),
        out_shape=jax.ShapeDtypeStruct(xs_zero.shape, xs_zero.dtype),
        input_output_aliases={4: 0},
        scratch_shapes=[pltpu.VMEM((tm, d), xs_zero.dtype), pltpu.SemaphoreType.DMA(())],
        compiler_params=_cparams(1),
        name="moe_dispatch",
    )(slots, x, mod, g, xs_zero)


def _experts_kernel(tile_expert_ref, n_used_ref, xs_ref, w13_ref, w2_ref, y_ref, xb_ref, acc_ref):
    del tile_expert_ref
    i, f = pl.program_id(0), pl.program_id(1)

    @pl.when(i < n_used_ref[0])
    def _():
        @pl.when(f == 0)
        def _():
            xb_ref[...] = xs_ref[...].astype(xb_ref.dtype)
            acc_ref[...] = jnp.zeros_like(acc_ref)

        acc_ref[...] += _swiglu_step(xb_ref[...], w13_ref, w2_ref)

        @pl.when(f == pl.num_programs(1) - 1)
        def _():
            y_ref[...] = acc_ref[...]

    @pl.when((i >= n_used_ref[0]) & (f == 0))
    def _():
        y_ref[...] = jnp.zeros_like(y_ref)


def _experts(tile_expert, n_used, xs, w13, w2, tm):
    p, d = xs.shape
    _, nf, _, tf2 = w13.shape
    tf = tf2 // 2
    grid_spec = pltpu.PrefetchScalarGridSpec(
        num_scalar_prefetch=2,
        grid=(p // tm, nf),
        in_specs=[pl.BlockSpec((tm, d), lambda i, f, te, nu: (i, 0)),
                  pl.BlockSpec((None, None, d, tf2), lambda i, f, te, nu: (te[i], f, 0, 0)),
                  pl.BlockSpec((None, None, tf, d), lambda i, f, te, nu: (te[i], f, 0, 0))],
        out_specs=pl.BlockSpec((tm, d), lambda i, f, te, nu: (i, 0)),
        scratch_shapes=[pltpu.VMEM((tm, d), BF16), pltpu.VMEM((tm, d), F32)],
    )
    return pl.pallas_call(
        _experts_kernel,
        grid_spec=grid_spec,
        out_shape=jax.ShapeDtypeStruct((p, d), F32),
        compiler_params=_cparams(2),
        name="moe_experts",
    )(tile_expert, n_used, xs, w13, w2)


def _combine_kernel(slots_ref, x_ref, info_ref, mod_ref, y_ref, o_ref, yb_ref, sem, *, geom):
    i = pl.program_id(0)
    tm, d = x_ref.shape

    def body(r, carry):
        for k in range(TOP_K):
            _row_copy(y_ref, slots_ref[TOP_K * r + k], yb_ref.at[k], r, sem).start()
        return carry

    lax.fori_loop(0, tm, body, 0)
    for k in range(TOP_K):
        pltpu.make_async_copy(y_ref.at[pl.ds(0, tm)], yb_ref.at[k], sem).wait()

    lane = lax.broadcasted_iota(jnp.int32, (SEG, INFO_COLS), 1)
    for s in range(geom["segs_per_tile"]):
        row = _mod_row(i, s, **geom)
        gate2 = _mod_vec(mod_ref, row, 5, d)
        sl = slice(s * SEG, (s + 1) * SEG)
        info = info_ref[sl, :]
        w1 = jnp.sum(jnp.where(lane == 2, info, 0.0), axis=1, keepdims=True)
        w2 = jnp.sum(jnp.where(lane == 3, info, 0.0), axis=1, keepdims=True)
        o_ref[sl, :] = x_ref[sl, :] + gate2 * (w1 * yb_ref[0, sl, :] + w2 * yb_ref[1, sl, :])


def _combine(slots, x, info, mod, y, geom, tm):
    t, d = x.shape
    return pl.pallas_call(
        functools.partial(_combine_kernel, geom=geom),
        grid=(t // tm,),
        in_specs=[pl.BlockSpec((TOP_K * tm,), lambda i: (i,), memory_space=pltpu.SMEM),
                  pl.BlockSpec((tm, d), lambda i: (i, 0)),
                  pl.BlockSpec((tm, INFO_COLS), lambda i: (i, 0)), _full(mod),
                  pl.BlockSpec(memory_space=pl.ANY)],
        out_specs=pl.BlockSpec((tm, d), lambda i: (i, 0)),
        out_shape=jax.ShapeDtypeStruct((t, d), F32),
        scratch_shapes=[pltpu.VMEM((TOP_K, tm, d), F32), pltpu.SemaphoreType.DMA(())],
        compiler_params=_cparams(1),
        name="moe_combine",
    )(slots, x, info, mod, y)


def _moe(x, mod, g, wr, w13, w2, geom, tm):
    t, d = x.shape
    n_e = wr.shape[1]
    info, counts = _router(x, mod, g, wr, geom, tm)
    counts = counts[0].astype(jnp.int32)
    padded = (counts + tm - 1) // tm * tm
    ends = jnp.cumsum(padded)
    starts = ends - padded
    n_tiles = TOP_K * t // tm + n_e
    tile_expert = jnp.minimum(jnp.searchsorted(ends, jnp.arange(n_tiles, dtype=jnp.int32) * tm, side="right"),
                              n_e - 1).astype(jnp.int32)
    n_used = (ends[-1:] // tm).astype(jnp.int32)
    experts = info[:, :TOP_K].astype(jnp.int32)
    ranks = info[:, 4:4 + TOP_K].astype(jnp.int32)
    slots = (starts[experts] + ranks).reshape(-1)
    xs = _dispatch(slots, x, mod, g, jnp.zeros((n_tiles * tm, d), F32), geom, tm)
    y = _experts(tile_expert, n_used, xs, w13, w2, tm)
    return _combine(slots, x, info, mod, y, geom, tm)


def _proj_b_kernel(x_ref, mod_ref, g_ref, wd_ref, qg_ref, kvg_ref, wq_ref, wk_ref, wv_ref,
                   cos_ref, sa_ref, sb_ref, q_ref, k_ref, v_ref, xn_ref, *, geom, scale):
    i = pl.program_id(0)
    _modulate_tile(x_ref, xn_ref, mod_ref, g_ref, i, 0, geom)
    xn = xn_ref[...]
    cos, sa, sb = cos_ref[...], sa_ref[...], sb_ref[...]
    dq = jnp.dot(xn, wd_ref[:, :B_Q_LORA], preferred_element_type=F32)
    dkv = jnp.dot(xn, wd_ref[:, B_Q_LORA:B_Q_LORA + B_KV_LORA], preferred_element_type=F32)
    kr = jnp.dot(xn, wd_ref[:, B_Q_LORA + B_KV_LORA:], preferred_element_type=F32)
    dqn = (_rms(dq) * qg_ref[...]).astype(BF16)
    dkvn = (_rms(dkv) * kvg_ref[...]).astype(BF16)
    kr = _rope(kr, cos, sa, sb, B_ROPE_DIM // 4).astype(k_ref.dtype)
    for h in range(B_HEADS):
        qh = jnp.dot(dqn, wq_ref[:, h * MXU_N:(h + 1) * MXU_N], preferred_element_type=F32)
        q_ref[:, h * MXU_N:h * MXU_N + LANES] = (qh[:, :LANES] * scale).astype(q_ref.dtype)
        q_ref[:, h * MXU_N + LANES:(h + 1) * MXU_N] = (
            _rope(qh[:, LANES:], cos, sa, sb, B_ROPE_DIM // 4) * scale).astype(q_ref.dtype)
        k_ref[:, h * MXU_N + LANES:(h + 1) * MXU_N] = kr
    for pair in range(B_HEADS // 2):
        sl = slice(pair * MXU_N, (pair + 1) * MXU_N)
        kn = jnp.dot(dkvn, wk_ref[:, sl], preferred_element_type=F32)
        for half in range(2):
            h = 2 * pair + half
            k_ref[:, h * MXU_N:h * MXU_N + LANES] = kn[:, half * LANES:(half + 1) * LANES].astype(k_ref.dtype)
        v_ref[:, sl] = jnp.dot(dkvn, wv_ref[:, sl], preferred_element_type=F32).astype(v_ref.dtype)


def _proj_b(x, mod, g, wd, qg, kvg, wq, wk, wv, tabs, geom, tm):
    t, d = x.shape
    rows = lambda width: pl.BlockSpec((tm, width), lambda i: (i, 0))
    nq, nv = B_HEADS * MXU_N, B_HEADS * B_V_DIM
    return pl.pallas_call(
        functools.partial(_proj_b_kernel, geom=geom, scale=(B_NOPE_DIM + B_ROPE_DIM) ** -0.5 * LOG2E),
        grid=(t // tm,),
        in_specs=[rows(d), _full(mod), _full(g), _full(wd), _full(qg), _full(kvg), _full(wq), _full(wk),
                  _full(wv), rows(LANES), rows(LANES), rows(LANES)],
        out_specs=[rows(nq), rows(nq), rows(nv)],
        out_shape=[jax.ShapeDtypeStruct((t, nq), BF16), jax.ShapeDtypeStruct((t, nq), BF16),
                   jax.ShapeDtypeStruct((t, nv), BF16)],
        scratch_shapes=[pltpu.VMEM((tm, d), BF16)],
        compiler_params=_cparams(1),
        name="proj_b",
    )(x, mod, g, wd, qg, kvg, wq, wk, wv, *tabs)


def _gelu(x):
    return 0.5 * x * (1.0 + lax.erf(x * (2.0 ** -0.5)))


def _sgu_kernel(x_ref, mod_ref, g_ref, win_ref, lng_ref, lnb_ref, ws_ref, bs_ref, wout_ref, o_ref,
                xn_ref, u_ref, v_ref, gated_ref, *, geom):
    i = pl.program_id(0)
    tm, d = x_ref.shape
    cw = u_ref.shape[1]
    _modulate_tile(x_ref, xn_ref, mod_ref, g_ref, i, 0, geom)
    xn = xn_ref[...]
    for j in range(cw // MXU_N):
        sl = slice(j * MXU_N, (j + 1) * MXU_N)
        u_ref[:, sl] = _gelu(jnp.dot(xn, win_ref[:, sl], preferred_element_type=F32))
        v_ref[:, sl] = _gelu(jnp.dot(xn, win_ref[:, cw + j * MXU_N:cw + (j + 1) * MXU_N],
                                     preferred_element_type=F32))
    lng, lnb = lng_ref[...], lnb_ref[...]
    gw = cw // C_GROUPS
    for c in range(tm // C_CHUNK):
        rs = slice(c * C_CHUNK, (c + 1) * C_CHUNK)
        v = v_ref[rs, :]
        mu = jnp.mean(v, axis=-1, keepdims=True)
        vc = v - mu
        var = jnp.mean(vc * vc, axis=-1, keepdims=True)
        vn = (vc * lax.rsqrt(var + EPS) * lng + lnb).astype(BF16)
        for gi in range(C_GROUPS):
            cs = slice(gi * gw, (gi + 1) * gw)
            mixed = jnp.dot(ws_ref[gi], vn[:, cs], preferred_element_type=F32) + bs_ref[:, cs]
            gated_ref[rs, cs] = (u_ref[rs, cs] * mixed).astype(gated_ref.dtype)
    for s in range(geom["segs_per_tile"]):
        row = _mod_row(i, s, **geom)
        gate = _mod_vec(mod_ref, row, 2, d)
        sl = slice(s * SEG, (s + 1) * SEG)
        y = jnp.dot(gated_ref[sl, :], wout_ref[...], preferred_element_type=F32)
        o_ref[sl, :] = x_ref[sl, :] + gate * y


def _sgu(x, mod, g, win, lng, lnb, ws, bs, wout, geom, tm):
    t, d = x.shape
    cw = wout.shape[0]
    return pl.pallas_call(
        functools.partial(_sgu_kernel, geom=geom),
        grid=(t // tm,),
        in_specs=[pl.BlockSpec((tm, d), lambda i: (i, 0)), _full(mod), _full(g), _full(win), _full(lng),
                  _full(lnb), _full(ws), _full(bs), _full(wout)],
        out_specs=pl.BlockSpec((tm, d), lambda i: (i, 0)),
        out_shape=jax.ShapeDtypeStruct((t, d), F32),
        scratch_shapes=[pltpu.VMEM((tm, d), BF16), pltpu.VMEM((tm, cw), F32), pltpu.VMEM((tm, cw), F32),
                        pltpu.VMEM((tm, cw), BF16)],
        compiler_params=_cparams(1),
        name="sgu",
    )(x, mod, g, win, lng, lnb, ws, bs, wout)


def _proj_d_kernel(x_ref, mod_ref, g_ref, w_ref, cos_ref, sa_ref, sb_ref, q_ref, k_ref, v_ref, xn_ref,
                   *, geom, scale):
    i = pl.program_id(0)
    _modulate_tile(x_ref, xn_ref, mod_ref, g_ref, i, 0, geom)
    xn = xn_ref[...]
    cos, sa, sb = cos_ref[...], sa_ref[...], sb_ref[...]
    nq = q_ref.shape[1]
    for pair in range(nq // MXU_N):
        y2 = jnp.dot(xn, w_ref[:, pair * MXU_N:(pair + 1) * MXU_N], preferred_element_type=F32)
        for half in range(2):
            sl = slice(half * LANES, (half + 1) * LANES)
            y = _rope(y2[:, sl], cos, sa, sb, D_HEAD_DIM // 4) * scale
            q_ref[:, pair * MXU_N + half * LANES:pair * MXU_N + (half + 1) * LANES] = y.astype(q_ref.dtype)
    kv = jnp.dot(xn, w_ref[:, nq:nq + MXU_N], preferred_element_type=F32)
    k_ref[...] = _rope(kv[:, :LANES], cos, sa, sb, D_HEAD_DIM // 4).astype(k_ref.dtype)
    v_ref[...] = kv[:, LANES:].astype(v_ref.dtype)


def _proj_d(x, mod, g, w, tabs, geom, tm):
    t, d = x.shape
    nq = D_HEADS * D_HEAD_DIM
    rows = lambda width: pl.BlockSpec((tm, width), lambda i: (i, 0))
    return pl.pallas_call(
        functools.partial(_proj_d_kernel, geom=geom, scale=D_HEAD_DIM ** -0.5),
        grid=(t // tm,),
        in_specs=[rows(d), _full(mod), _full(g), _full(w), rows(LANES), rows(LANES), rows(LANES)],
        out_specs=[rows(nq), rows(LANES), rows(LANES)],
        out_shape=[jax.ShapeDtypeStruct((t, nq), BF16), jax.ShapeDtypeStruct((t, LANES), BF16),
                   jax.ShapeDtypeStruct((t, LANES), BF16)],
        scratch_shapes=[pltpu.VMEM((tm, d), BF16)],
        compiler_params=_cparams(1),
        name="proj_d",
    )(x, mod, g, w, *tabs)


def _win_kernel(sink_ref, q_ref, kc_ref, km_ref, k0_ref, kp_ref, vc_ref, vm_ref, v0_ref, vp_ref, o_ref,
                *, lat_blocks, blocks_per_sample):
    j = pl.program_id(0)
    blk = q_ref.shape[0]
    ctx = kc_ref.shape[0]
    kk = jnp.concatenate([kc_ref[...], km_ref[...], k0_ref[...], kp_ref[...]], axis=0)
    vv = jnp.concatenate([vc_ref[...], vm_ref[...], v0_ref[...], vp_ref[...]], axis=0)
    n_keys = kk.shape[0]
    n = lax.rem(j, jnp.int32(blocks_per_sample))
    qpos = lax.broadcasted_iota(jnp.int32, (blk, n_keys), 0)
    col = lax.broadcasted_iota(jnp.int32, (blk, n_keys), 1)
    rel = col - ctx - blk - qpos
    kblock = n - 1 + lax.shift_right_arithmetic(col - ctx, int(math.log2(blk)))
    far = jnp.where(j < lat_blocks, 0, 2 * D_WINDOW + 2)
    in_window = (jnp.abs(rel) + far <= D_WINDOW) & (kblock >= 0) & (kblock < blocks_per_sample)
    mask = (col < ctx) | in_window
    group = D_HEADS // D_KV_HEADS
    for h in range(D_HEADS):
        hk = h // group
        q = q_ref[:, h * D_HEAD_DIM:(h + 1) * D_HEAD_DIM]
        k = kk[:, hk * D_HEAD_DIM:(hk + 1) * D_HEAD_DIM]
        v = vv[:, hk * D_HEAD_DIM:(hk + 1) * D_HEAD_DIM]
        s = lax.dot_general(q, k, (((1,), (1,)), ((), ())), preferred_element_type=F32)
        s = jnp.where(mask, s, MASK_VALUE)
        sink = sink_ref[h]
        m = jnp.maximum(jnp.max(s, axis=-1, keepdims=True), sink)
        e = jnp.exp(s - m)
        denom = jnp.sum(e, axis=-1, keepdims=True) + jnp.exp(sink - m)
        o = jnp.dot(e.astype(v.dtype), v, preferred_element_type=F32) / denom
        o_ref[:, h * D_HEAD_DIM:(h + 1) * D_HEAD_DIM] = o.astype(o_ref.dtype)


def _win_attn(sinks, q, k, v, *, n_batch, lat, ctx, blk=128):
    t, nq = q.shape
    bps = lat // blk
    lat_blocks = n_batch * bps
    ctx_bps = ctx // blk
    ctx_blk0 = n_batch * lat // ctx

    def sample(j):
        return jnp.where(j < lat_blocks, j // bps, (j - lat_blocks) // ctx_bps)

    def neighbour(j, off):
        b = sample(j)
        n = jnp.where(j < lat_blocks, j - b * bps, 0)
        return b * bps + jnp.clip(n + off, 0, bps - 1)

    kv_specs = [pl.BlockSpec((ctx, LANES), lambda j: (ctx_blk0 + sample(j), 0)),
                pl.BlockSpec((blk, LANES), lambda j: (neighbour(j, -1), 0)),
                pl.BlockSpec((blk, LANES), lambda j: (neighbour(j, 0), 0)),
                pl.BlockSpec((blk, LANES), lambda j: (neighbour(j, 1), 0))]
    return pl.pallas_call(
        functools.partial(_win_kernel, lat_blocks=lat_blocks, blocks_per_sample=bps),
        grid=(t // blk,),
        in_specs=[pl.BlockSpec(memory_space=pltpu.SMEM),
                  pl.BlockSpec((blk, nq), lambda j: (j, 0))] + kv_specs + kv_specs,
        out_specs=pl.BlockSpec((blk, nq), lambda j: (j, 0)),
        out_shape=jax.ShapeDtypeStruct((t, nq), BF16),
        compiler_params=_cparams(1),
        name="win_attn",
    )(sinks, q, k, k, k, k, v, v, v, v)


def _final_kernel(x_ref, g_ref, o_ref):
    o_ref[...] = _rms(x_ref[...]) * g_ref[...]


def _final_norm(x, g, n_rows, tm):
    d = x.shape[1]
    return pl.pallas_call(
        _final_kernel,
        grid=(n_rows // tm,),
        in_specs=[pl.BlockSpec((tm, d), lambda i: (i, 0)), _full(g)],
        out_specs=pl.BlockSpec((tm, d), lambda i: (i, 0)),
        out_shape=jax.ShapeDtypeStruct((n_rows, d), F32),
        compiler_params=_cparams(1),
        name="final_norm",
    )(x, g)


def _rope_tables(n_batch, lat, ctx, dim, tiled):
    quarter = dim // 4
    pos = jnp.arange(lat, dtype=jnp.int32)
    rows, cols = pos // GRID_W, pos % GRID_W
    inv_freq = ROPE_THETA ** (-jnp.arange(quarter, dtype=F32) / quarter)
    ang_r = rows.astype(F32)[:, None] * inv_freq
    ang_c = cols.astype(F32)[:, None] * inv_freq
    ang = jnp.concatenate([ang_r, ang_r, ang_c, ang_c], axis=-1)
    cos, sin = jnp.cos(ang), jnp.sin(ang)
    first = (jnp.arange(dim) % (2 * quarter)) < quarter
    out = []
    for tab, fill in ((cos, 1.0), (jnp.where(first, -sin, 0.0), 0.0), (jnp.where(first, 0.0, sin), 0.0)):
        if tiled:
            tab = jnp.tile(tab, (1, LANES // dim))
        else:
            tab = jnp.concatenate([tab, jnp.full((lat, LANES - dim), fill, F32)], axis=1)
        out.append(jnp.concatenate([jnp.tile(tab, (n_batch, 1)), jnp.full((n_batch * ctx, LANES), fill, F32)],
                                   axis=0))
    return tuple(out)


def _pick_tf(f, cap):
    return max(tf for tf in range(LANES, cap + 1, LANES) if f % tf == 0)


def kernel(x, c, ctx, c_ctx, ada_w, ada_b, norm_mix, norm_ffn, final_norm, a_wqkv, a_q_norm, a_k_norm, a_wo, b_w_down, b_q_lora_norm, b_kv_lora_norm, b_w_uq, b_w_ukv, b_wo, c_w_in, c_ln_g, c_ln_b, c_w_spatial, c_b_spatial, c_w_out, d_wqkv, d_sinks, d_wo, ffn_w13, ffn_w2, moe_router, moe_w13, moe_w2):
    n_batch, lat, d = x.shape
    n_ctx = ctx.shape[1]
    depth = ada_w.shape[0]
    assert n_ctx == SEG and lat % 1024 == 0 and n_batch < MOD_ROWS and depth == 4
    t = n_batch * (lat + n_ctx)
    segs_total = t // SEG
    segs_per_tile = max(k for k in (4, 2, 1) if segs_total % k == 0)
    tm = segs_per_tile * SEG
    geom = dict(segs_per_tile=segs_per_tile, lat_segs=n_batch * lat // SEG, segs_per_sample=lat // SEG,
                n_batch=n_batch)
    sgu_segs = 2 if segs_total % 2 == 0 else 1
    geom_sgu = dict(geom, segs_per_tile=sgu_segs)

    cvec = jnp.concatenate([c, c_ctx[None, :], jnp.zeros((MOD_ROWS - n_batch - 1, d), F32)], axis=0)
    mod = _ada_mod(cvec, ada_w, ada_b)
    h = jnp.concatenate([x.reshape(n_batch * lat, d), ctx.reshape(n_batch * n_ctx, d)], axis=0)

    tabs_a = _rope_tables(n_batch, lat, n_ctx, A_HEAD_DIM, True)
    tabs_d = _rope_tables(n_batch, lat, n_ctx, D_HEAD_DIM, True)
    tabs_b = _rope_tables(n_batch, lat, n_ctx, B_ROPE_DIM, False)
    row = lambda v: v.reshape(1, -1)
    dense_tf = _pick_tf(ffn_w13.shape[-1] // 2, 256)
    moe_tf = _pick_tf(moe_w13.shape[-1] // 2, 512)
    attn = functools.partial(_flash, n_batch=n_batch, lat=lat, ctx=n_ctx)

    qkv = _proj_a(h, mod[0], row(norm_mix[0]), a_wqkv[0].astype(BF16), row(a_q_norm[0]), row(a_k_norm[0]),
                  tabs_a, geom, tm)
    grp = A_HEADS // A_KV_HEADS
    o = attn(qkv, qkv, qkv, n_groups=A_KV_HEADS, heads=grp, dq=A_HEAD_DIM, dv=A_HEAD_DIM,
             k_col=lambda g: A_HEADS + g, v_col=lambda g: A_HEADS + A_KV_HEADS + g, tq=256)
    h = _proj_res(o, a_wo[0].astype(BF16), h, mod[0], geom, tm)
    h = _ffn(h, mod[0], row(norm_ffn[0]), _prep_w13(ffn_w13[0], dense_tf), _prep_w2(ffn_w2[0], dense_tf),
             geom, tm)

    wd = b_w_down[0]
    wd = jnp.concatenate([wd, jnp.zeros((d, LANES - B_ROPE_DIM), F32)], axis=1).astype(BF16)
    wq = b_w_uq[0].reshape(B_Q_LORA, B_HEADS, B_NOPE_DIM + B_ROPE_DIM)
    wq = jnp.concatenate([wq, jnp.zeros((B_Q_LORA, B_HEADS, MXU_N - B_NOPE_DIM - B_ROPE_DIM), F32)], axis=2)
    wq = wq.reshape(B_Q_LORA, B_HEADS * MXU_N).astype(BF16)
    wkv = b_w_ukv[0].reshape(B_KV_LORA, B_HEADS, B_NOPE_DIM + B_V_DIM)
    wk = wkv[:, :, :B_NOPE_DIM].reshape(B_KV_LORA, B_HEADS * B_NOPE_DIM).astype(BF16)
    wv = wkv[:, :, B_NOPE_DIM:].reshape(B_KV_LORA, B_HEADS * B_V_DIM).astype(BF16)
    qc, kc, vb = _proj_b(h, mod[1], row(norm_mix[1]), wd, row(b_q_lora_norm[0]), row(b_kv_lora_norm[0]),
                         wq, wk, wv, tabs_b, geom, tm)
    o = attn(qc, kc, vb, n_groups=B_HEADS, heads=1, dq=MXU_N, dv=B_V_DIM,
             k_col=lambda g: g, v_col=lambda g: g, tq=1024)
    h = _proj_res(o, b_wo[0].astype(BF16), h, mod[1], geom, tm)
    h = _moe(h, mod[1], row(norm_ffn[1]), moe_router[0], _prep_w13(moe_w13[0], moe_tf),
             _prep_w2(moe_w2[0], moe_tf), geom, tm)

    bs = jnp.repeat(c_b_spatial[0].T, c_w_out.shape[1] // C_GROUPS, axis=1)
    h = _sgu(h, mod[2], row(norm_mix[2]), c_w_in[0].astype(BF16), row(c_ln_g[0]), row(c_ln_b[0]),
             c_w_spatial[0].astype(BF16), bs, c_w_out[0].astype(BF16), geom_sgu, sgu_segs * SEG)
    h = _ffn(h, mod[2], row(norm_ffn[2]), _prep_w13(ffn_w13[1], dense_tf), _prep_w2(ffn_w2[1], dense_tf),
             geom, tm)

    q3, k3, v3 = _proj_d(h, mod[3], row(norm_mix[3]), d_wqkv[0].astype(BF16), tabs_d, geom, tm)
    o = _win_attn(d_sinks[0], q3, k3, v3, n_batch=n_batch, lat=lat, ctx=n_ctx)
    h = _proj_res(o, d_wo[0].astype(BF16), h, mod[3], geom, tm)
    h = _moe(h, mod[3], row(norm_ffn[3]), moe_router[1], _prep_w13(moe_w13[1], moe_tf),
             _prep_w2(moe_w2[1], moe_tf), geom, tm)

    out = _final_norm(h, row(final_norm), n_batch * lat, tm)
    return out.reshape(n_batch, lat, d)
```

```python
import functools
import math

import jax
import jax.numpy as jnp
from jax import lax
from jax.experimental import pallas as pl
from jax.experimental.pallas import tpu as pltpu

F32 = jnp.float32
BF16 = jnp.bfloat16

EPS = 1e-6
ROPE_THETA = 10000.0
GRID_W = 64
MASK_VALUE = -1e30
LOG2E = math.log2(math.e)

A_HEADS, A_KV_HEADS, A_HEAD_DIM = 8, 2, 128
B_HEADS, B_Q_LORA, B_KV_LORA, B_NOPE_DIM, B_ROPE_DIM, B_V_DIM = 8, 384, 256, 128, 64, 128
C_GROUPS, C_CHUNK = 8, 128
D_HEADS, D_KV_HEADS, D_HEAD_DIM, D_WINDOW = 16, 2, 64, 128
TOP_K = 2

SEG = 256
LANES = 128
MXU_N = 256
VMEM_LIMIT = 56 * 1024 * 1024

MOD_ROWS = 8
INFO_COLS = 8


def _cparams(n_axes):
    return pltpu.CompilerParams(dimension_semantics=("arbitrary",) * n_axes,
                                vmem_limit_bytes=VMEM_LIMIT)


def _full(a):
    return pl.BlockSpec(a.shape, lambda *_: (0,) * a.ndim)


def _mod_row(tile, s, segs_per_tile, lat_segs, segs_per_sample, n_batch):
    g = tile * segs_per_tile + s
    return jnp.where(g < lat_segs, lax.div(g, jnp.int32(segs_per_sample)), n_batch)


def _mod_vec(mod_ref, row, k, d):
    return mod_ref[pl.ds(row, 1), k * d:(k + 1) * d]


def _row_sum(x):
    return jnp.sum(x, axis=-1, keepdims=True)


def _row_max(x):
    return jnp.max(x, axis=-1, keepdims=True)


def _rms(x):
    return x * lax.rsqrt(jnp.mean(x * x, axis=-1, keepdims=True) + EPS)


def _modulate_tile(x_ref, xn_ref, mod_ref, g_ref, tile, k_shift, geom):
    d = x_ref.shape[-1]
    g = g_ref[...]
    for s in range(geom["segs_per_tile"]):
        row = _mod_row(tile, s, **geom)
        shift = _mod_vec(mod_ref, row, k_shift, d)
        scale = _mod_vec(mod_ref, row, k_shift + 1, d)
        x = x_ref[s * SEG:(s + 1) * SEG, :]
        y = _rms(x) * g * (1.0 + scale) + shift
        xn_ref[s * SEG:(s + 1) * SEG, :] = y.astype(xn_ref.dtype)


def _rope(t, cos, sin_a, sin_b, shift):
    return (t * cos + pltpu.roll(t, LANES - shift, axis=1) * sin_a
            + pltpu.roll(t, shift, axis=1) * sin_b)


def _ada_kernel(c_ref, w_ref, b_ref, o_ref):
    c = c_ref[...]
    a = c * jax.nn.sigmoid(c)
    o_ref[0] = jnp.dot(a, w_ref[0], preferred_element_type=F32,
                       precision=lax.Precision.HIGHEST) + b_ref[0]


def _ada_mod(cvec, ada_w, ada_b):
    depth, d, m = ada_w.shape
    tn = m // 4
    return pl.pallas_call(
        _ada_kernel,
        grid=(depth, m // tn),
        in_specs=[pl.BlockSpec((MOD_ROWS, d), lambda l, j: (0, 0)),
                  pl.BlockSpec((1, d, tn), lambda l, j: (l, 0, j)),
                  pl.BlockSpec((1, 1, tn), lambda l, j: (l, 0, j))],
        out_specs=pl.BlockSpec((1, MOD_ROWS, tn), lambda l, j: (l, 0, j)),
        out_shape=jax.ShapeDtypeStruct((depth, MOD_ROWS, m), F32),
        compiler_params=_cparams(2),
        name="ada_mod",
    )(cvec, ada_w, ada_b.reshape(depth, 1, m))


def _stream_tile(xl_ref, xc_ref, buf_ref, i, n_lat_tiles):
    @pl.when(i < n_lat_tiles)
    def _():
        buf_ref[...] = xl_ref[...]

    @pl.when(i >= n_lat_tiles)
    def _():
        buf_ref[...] = xc_ref[...]


def _stream_specs(tm, d, n_lat_tiles):
    return [pl.BlockSpec((tm, d), lambda i: (jnp.minimum(i, n_lat_tiles - 1), 0)),
            pl.BlockSpec((tm, d), lambda i: (jnp.maximum(i - n_lat_tiles, 0), 0))]


def _proj_a_kernel(xl_ref, xc_ref, mod_ref, g_ref, w_ref, qn_ref, kn_ref, cos_ref, sa_ref, sb_ref, o_ref,
                   x_ref, xn_ref, *, geom, scale, n_lat_tiles):
    i = pl.program_id(0)
    _stream_tile(xl_ref, xc_ref, x_ref, i, n_lat_tiles)
    _modulate_tile(x_ref, xn_ref, mod_ref, g_ref, i, 0, geom)
    xn = xn_ref[...]
    cos, sa, sb = cos_ref[...], sa_ref[...], sb_ref[...]
    qn, kn = qn_ref[...], kn_ref[...]
    n_q, n_k = A_HEADS, A_KV_HEADS
    n_heads = n_q + 2 * n_k
    for pair in range(n_heads // 2):
        y2 = jnp.dot(xn, w_ref[:, pair * MXU_N:(pair + 1) * MXU_N], preferred_element_type=F32)
        for half in range(2):
            h = 2 * pair + half
            y = y2[:, half * LANES:(half + 1) * LANES]
            if h < n_q:
                y = _rope(_rms(y) * qn, cos, sa, sb, A_HEAD_DIM // 4) * scale
            elif h < n_q + n_k:
                y = _rope(_rms(y) * kn, cos, sa, sb, A_HEAD_DIM // 4)
            o_ref[:, h * LANES:(h + 1) * LANES] = y.astype(o_ref.dtype)


def _proj_a(xl, xc, mod, g, w, qn, kn, tabs, geom, tm):
    d = xl.shape[1]
    t = xl.shape[0] + xc.shape[0]
    n = w.shape[1]
    n_lat_tiles = xl.shape[0] // tm
    rows = lambda width: pl.BlockSpec((tm, width), lambda i: (i, 0))
    return pl.pallas_call(
        functools.partial(_proj_a_kernel, geom=geom, scale=A_HEAD_DIM ** -0.5 * LOG2E, n_lat_tiles=n_lat_tiles),
        grid=(t // tm,),
        in_specs=_stream_specs(tm, d, n_lat_tiles) + [_full(mod), _full(g), _full(w), _full(qn), _full(kn),
                                                       rows(LANES), rows(LANES), rows(LANES)],
        out_specs=rows(n),
        out_shape=jax.ShapeDtypeStruct((t, n), BF16),
        scratch_shapes=[pltpu.VMEM((tm, d), F32), pltpu.VMEM((tm, d), BF16)],
        compiler_params=_cparams(1),
        name="proj_a",
    )(xl, xc, mod, g, w, qn, kn, *tabs)


def _flash_kernel(*refs, heads, dq, dv, tk, n_lat_chunks):
    if n_lat_chunks:
        _, q_ref, kc_ref, vc_ref, kl_ref, vl_ref, o_ref = refs
    else:
        _, q_ref, kc_ref, vc_ref, o_ref = refs
    tq = q_ref.shape[0]
    if heads > 1:
        q = jnp.concatenate([q_ref[:, g * dq:(g + 1) * dq] for g in range(heads)], axis=0)
    else:
        q = q_ref[...]
    rows = heads * tq

    def step(k, v, m, l, acc):
        s = lax.dot_general(q, k, (((1,), (1,)), ((), ())), preferred_element_type=F32)
        m_new = jnp.maximum(m, _row_max(s))
        alpha = jnp.exp2(m - m_new)
        p = jnp.exp2(s - m_new)
        l = alpha * l + _row_sum(p)
        acc = alpha * acc + jnp.dot(p.astype(v.dtype), v, preferred_element_type=F32)
        return m_new, l, acc

    m0 = jnp.full((rows, 1), MASK_VALUE, F32)
    l0 = jnp.zeros((rows, 1), F32)
    acc0 = jnp.zeros((rows, dv), F32)
    carry = step(kc_ref[...], vc_ref[...], m0, l0, acc0)
    if n_lat_chunks:
        def body(c, carry):
            off = pl.multiple_of(c * tk, tk)
            return step(kl_ref[pl.ds(off, tk), :], vl_ref[pl.ds(off, tk), :], *carry)
        carry = lax.fori_loop(0, n_lat_chunks, body, carry, unroll=8)
    m, l, acc = carry
    o = acc / l
    for g in range(heads):
        o_ref[:, g * dv:(g + 1) * dv] = o[g * tq:(g + 1) * tq, :].astype(o_ref.dtype)


def _flash(q, k, v, *, n_batch, lat, ctx, n_groups, heads, dq, dv, k_col, v_col, tq, tk=512):
    t = q.shape[0]
    nq = lat // tq
    ctx_blk0 = n_batch * lat // ctx
    kern = functools.partial(_flash_kernel, heads=heads, dq=dq, dv=dv, tk=tk)
    out_shape = jax.ShapeDtypeStruct((t, n_groups * heads * dv), BF16)
    o = pl.pallas_call(
        functools.partial(kern, n_lat_chunks=lat // tk),
        grid=(n_batch, n_groups, nq),
        in_specs=[pl.BlockSpec(memory_space=pl.ANY),
                  pl.BlockSpec((tq, heads * dq), lambda b, g, i: (b * nq + i, g)),
                  pl.BlockSpec((ctx, dq), lambda b, g, i: (ctx_blk0 + b, k_col(g))),
                  pl.BlockSpec((ctx, dv), lambda b, g, i: (ctx_blk0 + b, v_col(g))),
                  pl.BlockSpec((lat, dq), lambda b, g, i: (b, k_col(g))),
                  pl.BlockSpec((lat, dv), lambda b, g, i: (b, v_col(g)))],
        out_specs=pl.BlockSpec((tq, heads * dv), lambda b, g, i: (b * nq + i, g)),
        out_shape=out_shape,
        input_output_aliases={0: 0},
        compiler_params=_cparams(3),
        name="flash_lat",
    )(jnp.zeros(out_shape.shape, out_shape.dtype), q, k, v, k, v)
    return pl.pallas_call(
        functools.partial(kern, n_lat_chunks=0),
        grid=(n_batch, n_groups),
        in_specs=[pl.BlockSpec(memory_space=pl.ANY),
                  pl.BlockSpec((ctx, heads * dq), lambda b, g: (ctx_blk0 + b, g)),
                  pl.BlockSpec((ctx, dq), lambda b, g: (ctx_blk0 + b, k_col(g))),
                  pl.BlockSpec((ctx, dv), lambda b, g: (ctx_blk0 + b, v_col(g)))],
        out_specs=pl.BlockSpec((ctx, heads * dv), lambda b, g: (ctx_blk0 + b, g)),
        out_shape=out_shape,
        input_output_aliases={0: 0},
        compiler_params=_cparams(2),
        name="flash_ctx",
    )(o, q, k, v)


def _proj_res_kernel(a_ref, w_ref, *refs, geom, k_gate, n_lat_tiles):
    i = pl.program_id(0)
    if n_lat_tiles is None:
        h_ref, mod_ref, o_ref = refs
    else:
        hl_ref, hc_ref, mod_ref, o_ref, h_ref = refs
        _stream_tile(hl_ref, hc_ref, h_ref, i, n_lat_tiles)
    d = h_ref.shape[-1]
    for s in range(geom["segs_per_tile"]):
        row = _mod_row(i, s, **geom)
        gate = _mod_vec(mod_ref, row, k_gate, d)
        sl = slice(s * SEG, (s + 1) * SEG)
        y = jnp.dot(a_ref[sl, :], w_ref[...], preferred_element_type=F32)
        o_ref[sl, :] = h_ref[sl, :] + gate * y


def _proj_res(a, w, h, mod, geom, tm, k_gate=2):
    t, ka = a.shape
    two = isinstance(h, tuple)
    d = (h[0] if two else h).shape[1]
    n_lat_tiles = h[0].shape[0] // tm if two else None
    h_specs = _stream_specs(tm, d, n_lat_tiles) if two else [pl.BlockSpec((tm, d), lambda i: (i, 0))]
    return pl.pallas_call(
        functools.partial(_proj_res_kernel, geom=geom, k_gate=k_gate, n_lat_tiles=n_lat_tiles),
        grid=(t // tm,),
        in_specs=[pl.BlockSpec((tm, ka), lambda i: (i, 0)), _full(w)] + h_specs + [_full(mod)],
        out_specs=pl.BlockSpec((tm, d), lambda i: (i, 0)),
        out_shape=jax.ShapeDtypeStruct((t, d), F32),
        scratch_shapes=[pltpu.VMEM((tm, d), F32)] if two else [],
        compiler_params=_cparams(1),
        name="proj_res",
    )(a, w, *(h if two else (h,)), mod)


def _swiglu_step(xn, w1_ref, w3_ref, w2_ref):
    a = jnp.dot(xn, w1_ref[...], preferred_element_type=F32)
    b = jnp.dot(xn, w3_ref[...], preferred_element_type=F32)
    y = a * jax.nn.sigmoid(a) * b
    return jnp.dot(y.astype(BF16), w2_ref[...], preferred_element_type=F32)


def _ffn_kernel(x_ref, mod_ref, g_ref, w1_ref, w3_ref, w2_ref, o_ref, xn_ref, acc_ref, *, geom):
    i, f = pl.program_id(0), pl.program_id(1)
    d = x_ref.shape[-1]

    @pl.when(f == 0)
    def _():
        _modulate_tile(x_ref, xn_ref, mod_ref, g_ref, i, 3, geom)
        acc_ref[...] = jnp.zeros_like(acc_ref)

    acc_ref[...] += _swiglu_step(xn_ref[...], w1_ref, w3_ref, w2_ref)

    @pl.when(f == pl.num_programs(1) - 1)
    def _():
        for s in range(geom["segs_per_tile"]):
            row = _mod_row(i, s, **geom)
            gate2 = _mod_vec(mod_ref, row, 5, d)
            sl = slice(s * SEG, (s + 1) * SEG)
            o_ref[sl, :] = x_ref[sl, :] + gate2 * acc_ref[sl, :]


def _ffn(x, mod, g, w13, w2, geom, tm, tf):
    t, d = x.shape
    nf = w2.shape[0] // tf
    return pl.pallas_call(
        functools.partial(_ffn_kernel, geom=geom),
        grid=(t // tm, nf),
        in_specs=[pl.BlockSpec((tm, d), lambda i, f: (i, 0)), _full(mod), _full(g),
                  pl.BlockSpec((d, tf), lambda i, f: (0, f)),
                  pl.BlockSpec((d, tf), lambda i, f: (0, nf + f)),
                  pl.BlockSpec((tf, d), lambda i, f: (f, 0))],
        out_specs=pl.BlockSpec((tm, d), lambda i, f: (i, 0)),
        out_shape=jax.ShapeDtypeStruct((t, d), F32),
        scratch_shapes=[pltpu.VMEM((tm, d), BF16), pltpu.VMEM((tm, d), F32)],
        compiler_params=_cparams(2),
        name="ffn",
    )(x, mod, g, w13, w13, w2)


def _router_kernel(x_ref, mod_ref, g_ref, wr_ref, info_ref, cnt_ref, xn_ref, tri_ref, run_ref, *, geom):
    i = pl.program_id(0)
    tm = x_ref.shape[0]

    @pl.when(i == 0)
    def _():
        r = lax.broadcasted_iota(jnp.int32, (tm, tm), 0)
        c = lax.broadcasted_iota(jnp.int32, (tm, tm), 1)
        tri_ref[...] = jnp.where(c < r, 1.0, 0.0).astype(tri_ref.dtype)
        run_ref[...] = jnp.zeros_like(run_ref)

    _modulate_tile(x_ref, xn_ref, mod_ref, g_ref, i, 3, geom)
    logits = jnp.dot(xn_ref[...], wr_ref[...], preferred_element_type=F32, precision=lax.Precision.HIGHEST)
    n_e = float(logits.shape[1])
    lane = lax.broadcasted_iota(jnp.int32, logits.shape, 1).astype(F32)
    m1 = jnp.max(logits, axis=1, keepdims=True)
    i1 = jnp.min(jnp.where(logits == m1, lane, n_e), axis=1, keepdims=True)
    rest = jnp.where(lane == i1, -jnp.inf, logits)
    m2 = jnp.max(rest, axis=1, keepdims=True)
    i2 = jnp.min(jnp.where(rest == m2, lane, n_e), axis=1, keepdims=True)
    e2 = jnp.exp(m2 - m1)
    w1 = 1.0 / (1.0 + e2)
    w2 = e2 / (1.0 + e2)
    onehot = jnp.where((lane == i1) | (lane == i2), 1.0, 0.0)
    before = jnp.dot(tri_ref[...], onehot.astype(tri_ref.dtype), preferred_element_type=F32) + run_ref[...]
    r1 = jnp.sum(jnp.where(lane == i1, before, 0.0), axis=1, keepdims=True)
    r2 = jnp.sum(jnp.where(lane == i2, before, 0.0), axis=1, keepdims=True)
    run_ref[...] += jnp.sum(onehot, axis=0, keepdims=True)
    cnt_ref[...] = run_ref[...]
    vals = (i1, i2, w1, w2, r1, r2)
    info = jnp.zeros_like(logits)
    for k, val in enumerate(vals):
        info = jnp.where(lane == float(k), val, info)
    info_ref[...] = info


def _router(x, mod, g, wr, geom, tm):
    t, d = x.shape
    n_e = wr.shape[1]
    assert n_e == INFO_COLS
    return pl.pallas_call(
        functools.partial(_router_kernel, geom=geom),
        grid=(t // tm,),
        in_specs=[pl.BlockSpec((tm, d), lambda i: (i, 0)), _full(mod), _full(g), _full(wr)],
        out_specs=[pl.BlockSpec((tm, INFO_COLS), lambda i: (i, 0)),
                   pl.BlockSpec((1, n_e), lambda i: (0, 0))],
        out_shape=[jax.ShapeDtypeStruct((t, INFO_COLS), F32), jax.ShapeDtypeStruct((1, n_e), F32)],
        scratch_shapes=[pltpu.VMEM((tm, d), F32), pltpu.VMEM((tm, tm), BF16), pltpu.VMEM((1, n_e), F32)],
        compiler_params=_cparams(1),
        name="router",
    )(x, mod, g, wr)


DMA_ISSUE_UNROLL = 8


def _row_copy(src_ref, src_row, dst_ref, dst_row, sem):
    return pltpu.make_async_copy(src_ref.at[pl.ds(src_row, 1)], dst_ref.at[pl.ds(dst_row, 1)], sem)


def _dispatch_kernel(slots_ref, x_ref, mod_ref, g_ref, xs_in_ref, xs_ref, xn_ref, sem, *, geom):
    del xs_in_ref
    i = pl.program_id(0)
    tm = x_ref.shape[0]
    _modulate_tile(x_ref, xn_ref, mod_ref, g_ref, i, 3, geom)

    def body(r, carry):
        for k in range(TOP_K):
            _row_copy(xn_ref, r, xs_ref, slots_ref[TOP_K * r + k], sem).start(priority=k)
        return carry

    lax.fori_loop(0, tm, body, 0, unroll=DMA_ISSUE_UNROLL)
    for k in range(TOP_K):
        pltpu.make_async_copy(xn_ref, xs_ref.at[pl.ds(0, tm)], sem).wait()


def _dispatch(slots, x, mod, g, xs_zero, geom, tm):
    t, d = x.shape
    return pl.pallas_call(
        functools.partial(_dispatch_kernel, geom=geom),
        grid=(t // tm,),
        in_specs=[pl.BlockSpec((TOP_K * tm,), lambda i: (i,), memory_space=pltpu.SMEM),
                  pl.BlockSpec((tm, d), lambda i: (i, 0)), _full(mod), _full(g),
                  pl.BlockSpec(memory_space=pl.ANY)],
        out_specs=pl.BlockSpec(memory_space=pl.ANY),
        out_shape=jax.ShapeDtypeStruct(xs_zero.shape, xs_zero.dtype),
        input_output_aliases={4: 0},
        scratch_shapes=[pltpu.VMEM((tm, d), xs_zero.dtype), pltpu.SemaphoreType.DMA(())],
        compiler_params=_cparams(1),
        name="moe_dispatch",
    )(slots, x, mod, g, xs_zero)


def _experts_kernel(tile_expert_ref, n_used_ref, xs_ref, w1_ref, w3_ref, w2_ref, y_ref, xb_ref, acc_ref):
    del tile_expert_ref
    i, f = pl.program_id(0), pl.program_id(1)

    @pl.when(i < n_used_ref[0])
    def _():
        @pl.when(f == 0)
        def _():
            xb_ref[...] = xs_ref[...].astype(xb_ref.dtype)
            acc_ref[...] = jnp.zeros_like(acc_ref)

        acc_ref[...] += _swiglu_step(xb_ref[...], w1_ref, w3_ref, w2_ref)

        @pl.when(f == pl.num_programs(1) - 1)
        def _():
            y_ref[...] = acc_ref[...]

    @pl.when((i >= n_used_ref[0]) & (f == 0))
    def _():
        y_ref[...] = jnp.zeros_like(y_ref)


def _experts(tile_expert, n_used, xs, w13, w2, tm, tf):
    p, d = xs.shape
    nf = w2.shape[1] // tf
    grid_spec = pltpu.PrefetchScalarGridSpec(
        num_scalar_prefetch=2,
        grid=(p // tm, nf),
        in_specs=[pl.BlockSpec((tm, d), lambda i, f, te, nu: (i, 0)),
                  pl.BlockSpec((None, d, tf), lambda i, f, te, nu: (te[i], 0, f)),
                  pl.BlockSpec((None, d, tf), lambda i, f, te, nu: (te[i], 0, nf + f)),
                  pl.BlockSpec((None, tf, d), lambda i, f, te, nu: (te[i], f, 0))],
        out_specs=pl.BlockSpec((tm, d), lambda i, f, te, nu: (i, 0)),
        scratch_shapes=[pltpu.VMEM((tm, d), BF16), pltpu.VMEM((tm, d), F32)],
    )
    return pl.pallas_call(
        _experts_kernel,
        grid_spec=grid_spec,
        out_shape=jax.ShapeDtypeStruct((p, d), F32),
        compiler_params=_cparams(2),
        name="moe_experts",
    )(tile_expert, n_used, xs, w13, w13, w2)


def _combine_kernel(slots_ref, x_ref, info_ref, mod_ref, y_ref, o_ref, yb_ref, sem, *, geom):
    i = pl.program_id(0)
    tm, d = x_ref.shape

    def body(r, carry):
        for k in range(TOP_K):
            _row_copy(y_ref, slots_ref[TOP_K * r + k], yb_ref.at[k], r, sem).start(priority=k)
        return carry

    lax.fori_loop(0, tm, body, 0, unroll=DMA_ISSUE_UNROLL)
    for k in range(TOP_K):
        pltpu.make_async_copy(y_ref.at[pl.ds(0, tm)], yb_ref.at[k], sem).wait()

    lane = lax.broadcasted_iota(jnp.int32, (SEG, INFO_COLS), 1)
    for s in range(geom["segs_per_tile"]):
        row = _mod_row(i, s, **geom)
        gate2 = _mod_vec(mod_ref, row, 5, d)
        sl = slice(s * SEG, (s + 1) * SEG)
        info = info_ref[sl, :]
        w1 = jnp.sum(jnp.where(lane == 2, info, 0.0), axis=1, keepdims=True)
        w2 = jnp.sum(jnp.where(lane == 3, info, 0.0), axis=1, keepdims=True)
        o_ref[sl, :] = x_ref[sl, :] + gate2 * (w1 * yb_ref[0, sl, :] + w2 * yb_ref[1, sl, :])


def _combine(slots, x, info, mod, y, geom, tm):
    t, d = x.shape
    return pl.pallas_call(
        functools.partial(_combine_kernel, geom=geom),
        grid=(t // tm,),
        in_specs=[pl.BlockSpec((TOP_K * tm,), lambda i: (i,), memory_space=pltpu.SMEM),
                  pl.BlockSpec((tm, d), lambda i: (i, 0)),
                  pl.BlockSpec((tm, INFO_COLS), lambda i: (i, 0)), _full(mod),
                  pl.BlockSpec(memory_space=pl.ANY)],
        out_specs=pl.BlockSpec((tm, d), lambda i: (i, 0)),
        out_shape=jax.ShapeDtypeStruct((t, d), F32),
        scratch_shapes=[pltpu.VMEM((TOP_K, tm, d), F32), pltpu.SemaphoreType.DMA(())],
        compiler_params=_cparams(1),
        name="moe_combine",
    )(slots, x, info, mod, y)


def _moe(x, mod, g, wr, w13, w2, geom, tm, tf):
    t, d = x.shape
    n_e = wr.shape[1]
    info, counts = _router(x, mod, g, wr, geom, tm)
    counts = counts[0].astype(jnp.int32)
    padded = (counts + tm - 1) // tm * tm
    ends = jnp.cumsum(padded)
    starts = ends - padded
    n_tiles = TOP_K * t // tm + n_e
    tile_start = jnp.arange(n_tiles, dtype=jnp.int32) * tm
    tile_expert = jnp.minimum(jnp.sum((ends[None, :] <= tile_start[:, None]).astype(jnp.int32), axis=1), n_e - 1)
    n_used = (ends[-1:] // tm).astype(jnp.int32)
    experts = info[:, :TOP_K].astype(jnp.int32)
    ranks = info[:, 4:4 + TOP_K].astype(jnp.int32)
    slots = (starts[experts] + ranks).reshape(-1)
    xs = _dispatch(slots, x, mod, g, jnp.zeros((n_tiles * tm, d), F32), geom, tm)
    y = _experts(tile_expert, n_used, xs, w13, w2, tm, tf)
    return _combine(slots, x, info, mod, y, geom, tm)


def _proj_b_kernel(x_ref, mod_ref, g_ref, wd_ref, qg_ref, kvg_ref, wq_ref, wk_ref, wv_ref,
                   cos_ref, sa_ref, sb_ref, q_ref, k_ref, v_ref, xn_ref, *, geom, scale):
    i = pl.program_id(0)
    _modulate_tile(x_ref, xn_ref, mod_ref, g_ref, i, 0, geom)
    xn = xn_ref[...]
    cos, sa, sb = cos_ref[...], sa_ref[...], sb_ref[...]
    dq = jnp.dot(xn, wd_ref[:, :B_Q_LORA], preferred_element_type=F32)
    dkv = jnp.dot(xn, wd_ref[:, B_Q_LORA:B_Q_LORA + B_KV_LORA], preferred_element_type=F32)
    kr = jnp.dot(xn, wd_ref[:, B_Q_LORA + B_KV_LORA:], preferred_element_type=F32)
    dqn = (_rms(dq) * qg_ref[...]).astype(BF16)
    dkvn = (_rms(dkv) * kvg_ref[...]).astype(BF16)
    kr = _rope(kr, cos, sa, sb, B_ROPE_DIM // 4).astype(k_ref.dtype)
    for h in range(B_HEADS):
        qh = jnp.dot(dqn, wq_ref[:, h * MXU_N:(h + 1) * MXU_N], preferred_element_type=F32)
        q_ref[:, h * MXU_N:h * MXU_N + LANES] = (qh[:, :LANES] * scale).astype(q_ref.dtype)
        q_ref[:, h * MXU_N + LANES:(h + 1) * MXU_N] = (
            _rope(qh[:, LANES:], cos, sa, sb, B_ROPE_DIM // 4) * scale).astype(q_ref.dtype)
        k_ref[:, h * MXU_N + LANES:(h + 1) * MXU_N] = kr
    for pair in range(B_HEADS // 2):
        sl = slice(pair * MXU_N, (pair + 1) * MXU_N)
        kn = jnp.dot(dkvn, wk_ref[:, sl], preferred_element_type=F32)
        for half in range(2):
            h = 2 * pair + half
            k_ref[:, h * MXU_N:h * MXU_N + LANES] = kn[:, half * LANES:(half + 1) * LANES].astype(k_ref.dtype)
        v_ref[:, sl] = jnp.dot(dkvn, wv_ref[:, sl], preferred_element_type=F32).astype(v_ref.dtype)


def _proj_b(x, mod, g, wd, qg, kvg, wq, wk, wv, tabs, geom, tm):
    t, d = x.shape
    rows = lambda width: pl.BlockSpec((tm, width), lambda i: (i, 0))
    nq, nv = B_HEADS * MXU_N, B_HEADS * B_V_DIM
    return pl.pallas_call(
        functools.partial(_proj_b_kernel, geom=geom, scale=(B_NOPE_DIM + B_ROPE_DIM) ** -0.5 * LOG2E),
        grid=(t // tm,),
        in_specs=[rows(d), _full(mod), _full(g), _full(wd), _full(qg), _full(kvg), _full(wq), _full(wk),
                  _full(wv), rows(LANES), rows(LANES), rows(LANES)],
        out_specs=[rows(nq), rows(nq), rows(nv)],
        out_shape=[jax.ShapeDtypeStruct((t, nq), BF16), jax.ShapeDtypeStruct((t, nq), BF16),
                   jax.ShapeDtypeStruct((t, nv), BF16)],
        scratch_shapes=[pltpu.VMEM((tm, d), BF16)],
        compiler_params=_cparams(1),
        name="proj_b",
    )(x, mod, g, wd, qg, kvg, wq, wk, wv, *tabs)


def _gelu(x):
    return 0.5 * x * (1.0 + lax.erf(x * (2.0 ** -0.5)))


def _sgu_kernel(x_ref, mod_ref, g_ref, win_ref, lng_ref, lnb_ref, ws_ref, bs_ref, wout_ref, o_ref,
                xn_ref, u_ref, v_ref, gated_ref, *, geom):
    i = pl.program_id(0)
    tm, d = x_ref.shape
    cw = u_ref.shape[1]
    _modulate_tile(x_ref, xn_ref, mod_ref, g_ref, i, 0, geom)
    xn = xn_ref[...]
    for j in range(cw // MXU_N):
        sl = slice(j * MXU_N, (j + 1) * MXU_N)
        u_ref[:, sl] = _gelu(jnp.dot(xn, win_ref[:, sl], preferred_element_type=F32))
        v_ref[:, sl] = _gelu(jnp.dot(xn, win_ref[:, cw + j * MXU_N:cw + (j + 1) * MXU_N],
                                     preferred_element_type=F32))
    lng, lnb = lng_ref[...], lnb_ref[...]
    gw = cw // C_GROUPS
    for c in range(tm // C_CHUNK):
        rs = slice(c * C_CHUNK, (c + 1) * C_CHUNK)
        v = v_ref[rs, :]
        mu = _row_sum(v) * (1.0 / cw)
        vc = v - mu
        var = _row_sum(vc * vc) * (1.0 / cw)
        vn = (vc * lax.rsqrt(var + EPS) * lng + lnb).astype(BF16)
        for gi in range(C_GROUPS):
            cs = slice(gi * gw, (gi + 1) * gw)
            mixed = jnp.dot(ws_ref[gi], vn[:, cs], preferred_element_type=F32) + bs_ref[:, cs]
            gated_ref[rs, cs] = (u_ref[rs, cs] * mixed).astype(gated_ref.dtype)
    for s in range(geom["segs_per_tile"]):
        row = _mod_row(i, s, **geom)
        gate = _mod_vec(mod_ref, row, 2, d)
        sl = slice(s * SEG, (s + 1) * SEG)
        y = jnp.dot(gated_ref[sl, :], wout_ref[...], preferred_element_type=F32)
        o_ref[sl, :] = x_ref[sl, :] + gate * y


def _sgu(x, mod, g, win, lng, lnb, ws, bs, wout, geom, tm):
    t, d = x.shape
    cw = wout.shape[0]
    return pl.pallas_call(
        functools.partial(_sgu_kernel, geom=geom),
        grid=(t // tm,),
        in_specs=[pl.BlockSpec((tm, d), lambda i: (i, 0)), _full(mod), _full(g), _full(win), _full(lng),
                  _full(lnb), _full(ws), _full(bs), _full(wout)],
        out_specs=pl.BlockSpec((tm, d), lambda i: (i, 0)),
        out_shape=jax.ShapeDtypeStruct((t, d), F32),
        scratch_shapes=[pltpu.VMEM((tm, d), BF16), pltpu.VMEM((tm, cw), F32), pltpu.VMEM((tm, cw), F32),
                        pltpu.VMEM((tm, cw), BF16)],
        compiler_params=_cparams(1),
        name="sgu",
    )(x, mod, g, win, lng, lnb, ws, bs, wout)


def _proj_d_kernel(x_ref, mod_ref, g_ref, w_ref, cos_ref, sa_ref, sb_ref, q_ref, k_ref, v_ref, xn_ref,
                   *, geom, scale):
    i = pl.program_id(0)
    _modulate_tile(x_ref, xn_ref, mod_ref, g_ref, i, 0, geom)
    xn = xn_ref[...]
    cos, sa, sb = cos_ref[...], sa_ref[...], sb_ref[...]
    nq = q_ref.shape[1]
    for pair in range(nq // MXU_N):
        y2 = jnp.dot(xn, w_ref[:, pair * MXU_N:(pair + 1) * MXU_N], preferred_element_type=F32)
        for half in range(2):
            sl = slice(half * LANES, (half + 1) * LANES)
            y = _rope(y2[:, sl], cos, sa, sb, D_HEAD_DIM // 4) * scale
            q_ref[:, pair * MXU_N + half * LANES:pair * MXU_N + (half + 1) * LANES] = y.astype(q_ref.dtype)
    kv = jnp.dot(xn, w_ref[:, nq:nq + MXU_N], preferred_element_type=F32)
    k_ref[...] = _rope(kv[:, :LANES], cos, sa, sb, D_HEAD_DIM // 4).astype(k_ref.dtype)
    v_ref[...] = kv[:, LANES:].astype(v_ref.dtype)


def _proj_d(x, mod, g, w, tabs, geom, tm):
    t, d = x.shape
    nq = D_HEADS * D_HEAD_DIM
    rows = lambda width: pl.BlockSpec((tm, width), lambda i: (i, 0))
    return pl.pallas_call(
        functools.partial(_proj_d_kernel, geom=geom, scale=D_HEAD_DIM ** -0.5 * LOG2E),
        grid=(t // tm,),
        in_specs=[rows(d), _full(mod), _full(g), _full(w), rows(LANES), rows(LANES), rows(LANES)],
        out_specs=[rows(nq), rows(LANES), rows(LANES)],
        out_shape=[jax.ShapeDtypeStruct((t, nq), BF16), jax.ShapeDtypeStruct((t, LANES), BF16),
                   jax.ShapeDtypeStruct((t, LANES), BF16)],
        scratch_shapes=[pltpu.VMEM((tm, d), BF16)],
        compiler_params=_cparams(1),
        name="proj_d",
    )(x, mod, g, w, *tabs)


def _win_kernel(sink_ref, q_ref, kc_ref, km_ref, k0_ref, kp_ref, vc_ref, vm_ref, v0_ref, vp_ref, o_ref,
                *, lat_blocks, blocks_per_sample):
    j = pl.program_id(0)
    blk = q_ref.shape[0]
    ctx = kc_ref.shape[0]
    kk = jnp.concatenate([kc_ref[...], km_ref[...], k0_ref[...], kp_ref[...]], axis=0)
    vv = jnp.concatenate([vc_ref[...], vm_ref[...], v0_ref[...], vp_ref[...]], axis=0)
    n_keys = kk.shape[0]
    n = lax.rem(j, jnp.int32(blocks_per_sample))
    qpos = lax.broadcasted_iota(jnp.int32, (blk, n_keys), 0)
    col = lax.broadcasted_iota(jnp.int32, (blk, n_keys), 1)
    rel = col - ctx - blk - qpos
    kblock = n - 1 + lax.shift_right_arithmetic(col - ctx, int(math.log2(blk)))
    far = jnp.where(j < lat_blocks, 0, 2 * D_WINDOW + 2)
    in_window = (jnp.abs(rel) + far <= D_WINDOW) & (kblock >= 0) & (kblock < blocks_per_sample)
    mask = (col < ctx) | in_window
    group = D_HEADS // D_KV_HEADS
    bias = jnp.concatenate([jnp.where(mask, 0.0, MASK_VALUE)] * group, axis=0)
    for hk in range(D_KV_HEADS):
        heads = range(hk * group, (hk + 1) * group)
        q = jnp.concatenate([q_ref[:, h * D_HEAD_DIM:(h + 1) * D_HEAD_DIM] for h in heads], axis=0)
        sink = jnp.concatenate([jnp.full((blk, 1), sink_ref[h] * LOG2E, F32) for h in heads], axis=0)
        k = kk[:, hk * D_HEAD_DIM:(hk + 1) * D_HEAD_DIM]
        v = vv[:, hk * D_HEAD_DIM:(hk + 1) * D_HEAD_DIM]
        s = lax.dot_general(q, k, (((1,), (1,)), ((), ())), preferred_element_type=F32) + bias
        m = jnp.maximum(_row_max(s), sink)
        e = jnp.exp2(s - m)
        denom = _row_sum(e) + jnp.exp2(sink - m)
        o = jnp.dot(e.astype(v.dtype), v, preferred_element_type=F32) / denom
        for n_h, h in enumerate(heads):
            o_ref[:, h * D_HEAD_DIM:(h + 1) * D_HEAD_DIM] = o[n_h * blk:(n_h + 1) * blk, :].astype(o_ref.dtype)


def _win_attn(sinks, q, k, v, *, n_batch, lat, ctx, blk=128):
    t, nq = q.shape
    bps = lat // blk
    lat_blocks = n_batch * bps
    ctx_bps = ctx // blk
    ctx_blk0 = n_batch * lat // ctx

    def sample(j):
        return jnp.where(j < lat_blocks, j // bps, (j - lat_blocks) // ctx_bps)

    def neighbour(j, off):
        b = sample(j)
        n = jnp.where(j < lat_blocks, j - b * bps, 0)
        return b * bps + jnp.clip(n + off, 0, bps - 1)

    kv_specs = [pl.BlockSpec((ctx, LANES), lambda j: (ctx_blk0 + sample(j), 0)),
                pl.BlockSpec((blk, LANES), lambda j: (neighbour(j, -1), 0)),
                pl.BlockSpec((blk, LANES), lambda j: (neighbour(j, 0), 0)),
                pl.BlockSpec((blk, LANES), lambda j: (neighbour(j, 1), 0))]
    return pl.pallas_call(
        functools.partial(_win_kernel, lat_blocks=lat_blocks, blocks_per_sample=bps),
        grid=(t // blk,),
        in_specs=[pl.BlockSpec(memory_space=pltpu.SMEM),
                  pl.BlockSpec((blk, nq), lambda j: (j, 0))] + kv_specs + kv_specs,
        out_specs=pl.BlockSpec((blk, nq), lambda j: (j, 0)),
        out_shape=jax.ShapeDtypeStruct((t, nq), BF16),
        compiler_params=_cparams(1),
        name="win_attn",
    )(sinks, q, k, k, k, k, v, v, v, v)


def _final_kernel(x_ref, g_ref, o_ref):
    o_ref[...] = _rms(x_ref[...]) * g_ref[...]


def _final_norm(x, g, n_rows, tm):
    d = x.shape[1]
    return pl.pallas_call(
        _final_kernel,
        grid=(n_rows // tm,),
        in_specs=[pl.BlockSpec((tm, d), lambda i: (i, 0)), _full(g)],
        out_specs=pl.BlockSpec((tm, d), lambda i: (i, 0)),
        out_shape=jax.ShapeDtypeStruct((n_rows, d), F32),
        compiler_params=_cparams(1),
        name="final_norm",
    )(x, g)


def _rope_tables(n_batch, lat, ctx, dim, tiled):
    quarter = dim // 4
    pos = jnp.arange(lat, dtype=jnp.int32)
    rows, cols = pos // GRID_W, pos % GRID_W
    inv_freq = ROPE_THETA ** (-jnp.arange(quarter, dtype=F32) / quarter)
    ang_r = rows.astype(F32)[:, None] * inv_freq
    ang_c = cols.astype(F32)[:, None] * inv_freq
    ang = jnp.concatenate([ang_r, ang_r, ang_c, ang_c], axis=-1)
    cos, sin = jnp.cos(ang), jnp.sin(ang)
    first = (jnp.arange(dim) % (2 * quarter)) < quarter
    out = []
    for tab, fill in ((cos, 1.0), (jnp.where(first, -sin, 0.0), 0.0), (jnp.where(first, 0.0, sin), 0.0)):
        if tiled:
            tab = jnp.tile(tab, (1, LANES // dim))
        else:
            tab = jnp.concatenate([tab, jnp.full((lat, LANES - dim), fill, F32)], axis=1)
        out.append(jnp.concatenate([jnp.tile(tab, (n_batch, 1)), jnp.full((n_batch * ctx, LANES), fill, F32)],
                                   axis=0))
    return tuple(out)


def _pick_tf(f, cap):
    return max(tf for tf in range(LANES, cap + 1, LANES) if f % tf == 0)


def kernel(x, c, ctx, c_ctx, ada_w, ada_b, norm_mix, norm_ffn, final_norm, a_wqkv, a_q_norm, a_k_norm, a_wo, b_w_down, b_q_lora_norm, b_kv_lora_norm, b_w_uq, b_w_ukv, b_wo, c_w_in, c_ln_g, c_ln_b, c_w_spatial, c_b_spatial, c_w_out, d_wqkv, d_sinks, d_wo, ffn_w13, ffn_w2, moe_router, moe_w13, moe_w2):
    n_batch, lat, d = x.shape
    n_ctx = ctx.shape[1]
    depth = ada_w.shape[0]
    assert n_ctx == SEG and lat % 1024 == 0 and n_batch < MOD_ROWS and depth == 4
    ctx_segs = n_batch * n_ctx // SEG
    segs_per_tile = max(k for k in (4, 2, 1) if ctx_segs % k == 0)
    tm = segs_per_tile * SEG
    geom = dict(segs_per_tile=segs_per_tile, lat_segs=n_batch * lat // SEG, segs_per_sample=lat // SEG,
                n_batch=n_batch)
    sgu_segs = min(2, segs_per_tile)
    geom_sgu = dict(geom, segs_per_tile=sgu_segs)

    cvec = jnp.concatenate([c, c_ctx[None, :], jnp.zeros((MOD_ROWS - n_batch - 1, d), F32)], axis=0)
    mod = _ada_mod(cvec, ada_w, ada_b)
    h = (x.reshape(n_batch * lat, d), ctx.reshape(n_batch * n_ctx, d))

    tabs_a = _rope_tables(n_batch, lat, n_ctx, A_HEAD_DIM, True)
    tabs_d = _rope_tables(n_batch, lat, n_ctx, D_HEAD_DIM, True)
    tabs_b = _rope_tables(n_batch, lat, n_ctx, B_ROPE_DIM, False)
    row = lambda v: v.reshape(1, -1)
    dense_tf = _pick_tf(ffn_w13.shape[-1] // 2, 256)
    moe_tf = _pick_tf(moe_w13.shape[-1] // 2, 512)
    attn = functools.partial(_flash, n_batch=n_batch, lat=lat, ctx=n_ctx)

    qkv = _proj_a(*h, mod[0], row(norm_mix[0]), a_wqkv[0].astype(BF16), row(a_q_norm[0]), row(a_k_norm[0]),
                  tabs_a, geom, tm)
    grp = A_HEADS // A_KV_HEADS
    o = attn(qkv, qkv, qkv, n_groups=A_KV_HEADS, heads=grp, dq=A_HEAD_DIM, dv=A_HEAD_DIM,
             k_col=lambda g: A_HEADS + g, v_col=lambda g: A_HEADS + A_KV_HEADS + g, tq=256)
    h = _proj_res(o, a_wo[0].astype(BF16), h, mod[0], geom, tm)
    h = _ffn(h, mod[0], row(norm_ffn[0]), ffn_w13[0].astype(BF16), ffn_w2[0].astype(BF16), geom, tm, dense_tf)

    wd = b_w_down[0]
    wd = jnp.concatenate([wd, jnp.zeros((d, LANES - B_ROPE_DIM), F32)], axis=1).astype(BF16)
    wq = b_w_uq[0].reshape(B_Q_LORA, B_HEADS, B_NOPE_DIM + B_ROPE_DIM)
    wq = jnp.concatenate([wq, jnp.zeros((B_Q_LORA, B_HEADS, MXU_N - B_NOPE_DIM - B_ROPE_DIM), F32)], axis=2)
    wq = wq.reshape(B_Q_LORA, B_HEADS * MXU_N).astype(BF16)
    wkv = b_w_ukv[0].reshape(B_KV_LORA, B_HEADS, B_NOPE_DIM + B_V_DIM)
    wk = wkv[:, :, :B_NOPE_DIM].reshape(B_KV_LORA, B_HEADS * B_NOPE_DIM).astype(BF16)
    wv = wkv[:, :, B_NOPE_DIM:].reshape(B_KV_LORA, B_HEADS * B_V_DIM).astype(BF16)
    qc, kc, vb = _proj_b(h, mod[1], row(norm_mix[1]), wd, row(b_q_lora_norm[0]), row(b_kv_lora_norm[0]),
                         wq, wk, wv, tabs_b, geom, tm)
    o = attn(qc, kc, vb, n_groups=B_HEADS, heads=1, dq=MXU_N, dv=B_V_DIM,
             k_col=lambda g: g, v_col=lambda g: g, tq=1024)
    h = _proj_res(o, b_wo[0].astype(BF16), h, mod[1], geom, tm)
    h = _moe(h, mod[1], row(norm_ffn[1]), moe_router[0], moe_w13[0].astype(BF16), moe_w2[0].astype(BF16),
             geom, tm, moe_tf)

    bs = jnp.repeat(c_b_spatial[0].T, c_w_out.shape[1] // C_GROUPS, axis=1)
    h = _sgu(h, mod[2], row(norm_mix[2]), c_w_in[0].astype(BF16), row(c_ln_g[0]), row(c_ln_b[0]),
             c_w_spatial[0].astype(BF16), bs, c_w_out[0].astype(BF16), geom_sgu, sgu_segs * SEG)
    h = _ffn(h, mod[2], row(norm_ffn[2]), ffn_w13[1].astype(BF16), ffn_w2[1].astype(BF16), geom, tm, dense_tf)

    q3, k3, v3 = _proj_d(h, mod[3], row(norm_mix[3]), d_wqkv[0].astype(BF16), tabs_d, geom, tm)
    o = _win_attn(d_sinks[0], q3, k3, v3, n_batch=n_batch, lat=lat, ctx=n_ctx)
    h = _proj_res(o, d_wo[0].astype(BF16), h, mod[3], geom, tm)
    h = _moe(h, mod[3], row(norm_ffn[3]), moe_router[1], moe_w13[1].astype(BF16), moe_w2[1].astype(BF16),
             geom, tm, moe_tf)

    out = _final_norm(h, row(final_norm), n_batch * lat, tm)
    return out.reshape(n_batch, lat, d)
```

```python
import functools
import math

import jax
import jax.numpy as jnp
from jax import lax
from jax.experimental import pallas as pl
from jax.experimental.pallas import tpu as pltpu

F32 = jnp.float32
BF16 = jnp.bfloat16

EPS = 1e-6
ROPE_THETA = 10000.0
GRID_W = 64
MASK_VALUE = -1e30
LOG2E = math.log2(math.e)

A_HEADS, A_KV_HEADS, A_HEAD_DIM = 8, 2, 128
B_HEADS, B_Q_LORA, B_KV_LORA, B_NOPE_DIM, B_ROPE_DIM, B_V_DIM = 8, 384, 256, 128, 64, 128
C_GROUPS, C_CHUNK = 8, 128
D_HEADS, D_KV_HEADS, D_HEAD_DIM, D_WINDOW = 16, 2, 64, 128
TOP_K = 2

SEG = 256
LANES = 128
MXU_N = 256
VMEM_LIMIT = 56 * 1024 * 1024

MOD_ROWS = 8
INFO_COLS = 8


def _cparams(n_axes):
    return pltpu.CompilerParams(dimension_semantics=("arbitrary",) * n_axes,
                                vmem_limit_bytes=VMEM_LIMIT)


def _full(a):
    return pl.BlockSpec(a.shape, lambda *_: (0,) * a.ndim)


def _mod_row(tile, s, segs_per_tile, lat_segs, segs_per_sample, n_batch):
    g = tile * segs_per_tile + s
    return jnp.where(g < lat_segs, lax.div(g, jnp.int32(segs_per_sample)), n_batch)


def _mod_vec(mod_ref, row, k, d):
    return mod_ref[pl.ds(row, 1), k * d:(k + 1) * d]


def _row_sum(x):
    return jnp.sum(x, axis=-1, keepdims=True)


def _row_max(x):
    return jnp.max(x, axis=-1, keepdims=True)


def _rms(x):
    return x * lax.rsqrt(jnp.mean(x * x, axis=-1, keepdims=True) + EPS)


def _modulate_tile(x_ref, xn_ref, mod_ref, g_ref, tile, k_shift, geom):
    d = x_ref.shape[-1]
    g = g_ref[...]
    for s in range(geom["segs_per_tile"]):
        row = _mod_row(tile, s, **geom)
        shift = _mod_vec(mod_ref, row, k_shift, d)
        scale = _mod_vec(mod_ref, row, k_shift + 1, d)
        x = x_ref[s * SEG:(s + 1) * SEG, :]
        y = _rms(x) * g * (1.0 + scale) + shift
        xn_ref[s * SEG:(s + 1) * SEG, :] = y.astype(xn_ref.dtype)


def _rope(t, cos, sin_a, sin_b, shift):
    return (t * cos + pltpu.roll(t, LANES - shift, axis=1) * sin_a
            + pltpu.roll(t, shift, axis=1) * sin_b)


def _ada_kernel(c_ref, w_ref, b_ref, o_ref):
    c = c_ref[...]
    a = c * jax.nn.sigmoid(c)
    o_ref[0] = jnp.dot(a, w_ref[0], preferred_element_type=F32,
                       precision=lax.Precision.HIGHEST) + b_ref[0]


def _ada_mod(cvec, ada_w, ada_b):
    depth, d, m = ada_w.shape
    tn = m // 4
    return pl.pallas_call(
        _ada_kernel,
        grid=(depth, m // tn),
        in_specs=[pl.BlockSpec((MOD_ROWS, d), lambda l, j: (0, 0)),
                  pl.BlockSpec((1, d, tn), lambda l, j: (l, 0, j)),
                  pl.BlockSpec((1, 1, tn), lambda l, j: (l, 0, j))],
        out_specs=pl.BlockSpec((1, MOD_ROWS, tn), lambda l, j: (l, 0, j)),
        out_shape=jax.ShapeDtypeStruct((depth, MOD_ROWS, m), F32),
        compiler_params=_cparams(2),
        name="ada_mod",
    )(cvec, ada_w, ada_b.reshape(depth, 1, m))


def _stream_tile(xl_ref, xc_ref, buf_ref, i, n_lat_tiles):
    @pl.when(i < n_lat_tiles)
    def _():
        buf_ref[...] = xl_ref[...]

    @pl.when(i >= n_lat_tiles)
    def _():
        buf_ref[...] = xc_ref[...]


def _stream_specs(tm, d, n_lat_tiles):
    return [pl.BlockSpec((tm, d), lambda i: (jnp.minimum(i, n_lat_tiles - 1), 0)),
            pl.BlockSpec((tm, d), lambda i: (jnp.maximum(i - n_lat_tiles, 0), 0))]


def _proj_a_kernel(xl_ref, xc_ref, mod_ref, g_ref, w_ref, qn_ref, kn_ref, cos_ref, sa_ref, sb_ref, o_ref,
                   x_ref, xn_ref, *, geom, scale, n_lat_tiles):
    i = pl.program_id(0)
    _stream_tile(xl_ref, xc_ref, x_ref, i, n_lat_tiles)
    _modulate_tile(x_ref, xn_ref, mod_ref, g_ref, i, 0, geom)
    xn = xn_ref[...]
    cos, sa, sb = cos_ref[...], sa_ref[...], sb_ref[...]
    qn, kn = qn_ref[...], kn_ref[...]
    n_q, n_k = A_HEADS, A_KV_HEADS
    n_heads = n_q + 2 * n_k
    for pair in range(n_heads // 2):
        y2 = jnp.dot(xn, w_ref[:, pair * MXU_N:(pair + 1) * MXU_N], preferred_element_type=F32)
        for half in range(2):
            h = 2 * pair + half
            y = y2[:, half * LANES:(half + 1) * LANES]
            if h < n_q:
                y = _rope(_rms(y) * qn, cos, sa, sb, A_HEAD_DIM // 4) * scale
            elif h < n_q + n_k:
                y = _rope(_rms(y) * kn, cos, sa, sb, A_HEAD_DIM // 4)
            o_ref[:, h * LANES:(h + 1) * LANES] = y.astype(o_ref.dtype)


def _proj_a(xl, xc, mod, g, w, qn, kn, tabs, geom, tm):
    d = xl.shape[1]
    t = xl.shape[0] + xc.shape[0]
    n = w.shape[1]
    n_lat_tiles = xl.shape[0] // tm
    rows = lambda width: pl.BlockSpec((tm, width), lambda i: (i, 0))
    return pl.pallas_call(
        functools.partial(_proj_a_kernel, geom=geom, scale=A_HEAD_DIM ** -0.5 * LOG2E, n_lat_tiles=n_lat_tiles),
        grid=(t // tm,),
        in_specs=_stream_specs(tm, d, n_lat_tiles) + [_full(mod), _full(g), _full(w), _full(qn), _full(kn),
                                                       rows(LANES), rows(LANES), rows(LANES)],
        out_specs=rows(n),
        out_shape=jax.ShapeDtypeStruct((t, n), BF16),
        scratch_shapes=[pltpu.VMEM((tm, d), F32), pltpu.VMEM((tm, d), BF16)],
        compiler_params=_cparams(1),
        name="proj_a",
    )(xl, xc, mod, g, w, qn, kn, *tabs)


FLASH_UNROLL = 8


def _flash_kernel(*refs, heads, dq, dv, tk, n_lat_chunks):
    if n_lat_chunks:
        _, q_ref, kc_ref, vc_ref, kl_ref, vl_ref, o_ref = refs
    else:
        _, q_ref, kc_ref, vc_ref, o_ref = refs
    tq = q_ref.shape[0]
    if heads > 1:
        q = jnp.concatenate([q_ref[:, g * dq:(g + 1) * dq] for g in range(heads)], axis=0)
    else:
        q = q_ref[...]
    rows = heads * tq

    def scores(k):
        return lax.dot_general(q, k, (((1,), (1,)), ((), ())), preferred_element_type=F32)

    def update(s, v, m, acc):
        m_new = jnp.maximum(m, _row_max(s))
        alpha = jnp.exp2(m - m_new)
        p = jnp.exp2((s - m_new).astype(v.dtype))
        v_ones = jnp.concatenate([v, jnp.ones_like(v)], axis=1)
        acc = alpha * acc + jnp.dot(p, v_ones, preferred_element_type=F32)
        return m_new, acc

    m0 = jnp.full((rows, 1), MASK_VALUE, F32)
    acc0 = jnp.zeros((rows, 2 * dv), F32)
    m, acc = update(scores(kc_ref[...]), vc_ref[...], m0, acc0)
    if n_lat_chunks:
        def body(c, carry):
            off = pl.multiple_of(c * tk, tk)
            return update(scores(kl_ref[pl.ds(off, tk), :]), vl_ref[pl.ds(off, tk), :], *carry)

        m, acc = lax.fori_loop(0, n_lat_chunks, body, (m, acc), unroll=FLASH_UNROLL)
    o = acc[:, :dv] / acc[:, dv:]
    for g in range(heads):
        o_ref[:, g * dv:(g + 1) * dv] = o[g * tq:(g + 1) * tq, :].astype(o_ref.dtype)


def _flash(q, k, v, *, n_batch, lat, ctx, n_groups, heads, dq, dv, k_col, v_col, tq, tk=512):
    t = q.shape[0]
    nq = lat // tq
    ctx_blk0 = n_batch * lat // ctx
    kern = functools.partial(_flash_kernel, heads=heads, dq=dq, dv=dv, tk=tk)
    out_shape = jax.ShapeDtypeStruct((t, n_groups * heads * dv), BF16)
    o = pl.pallas_call(
        functools.partial(kern, n_lat_chunks=lat // tk),
        grid=(n_batch, n_groups, nq),
        in_specs=[pl.BlockSpec(memory_space=pl.ANY),
                  pl.BlockSpec((tq, heads * dq), lambda b, g, i: (b * nq + i, g)),
                  pl.BlockSpec((ctx, dq), lambda b, g, i: (ctx_blk0 + b, k_col(g))),
                  pl.BlockSpec((ctx, dv), lambda b, g, i: (ctx_blk0 + b, v_col(g))),
                  pl.BlockSpec((lat, dq), lambda b, g, i: (b, k_col(g))),
                  pl.BlockSpec((lat, dv), lambda b, g, i: (b, v_col(g)))],
        out_specs=pl.BlockSpec((tq, heads * dv), lambda b, g, i: (b * nq + i, g)),
        out_shape=out_shape,
        input_output_aliases={0: 0},
        compiler_params=_cparams(3),
        name="flash_lat",
    )(jnp.zeros(out_shape.shape, out_shape.dtype), q, k, v, k, v)
    return pl.pallas_call(
        functools.partial(kern, n_lat_chunks=0),
        grid=(n_batch, n_groups),
        in_specs=[pl.BlockSpec(memory_space=pl.ANY),
                  pl.BlockSpec((ctx, heads * dq), lambda b, g: (ctx_blk0 + b, g)),
                  pl.BlockSpec((ctx, dq), lambda b, g: (ctx_blk0 + b, k_col(g))),
                  pl.BlockSpec((ctx, dv), lambda b, g: (ctx_blk0 + b, v_col(g)))],
        out_specs=pl.BlockSpec((ctx, heads * dv), lambda b, g: (ctx_blk0 + b, g)),
        out_shape=out_shape,
        input_output_aliases={0: 0},
        compiler_params=_cparams(2),
        name="flash_ctx",
    )(o, q, k, v)


def _proj_res_kernel(a_ref, w_ref, *refs, geom, k_gate, n_lat_tiles):
    i = pl.program_id(0)
    if n_lat_tiles is None:
        h_ref, mod_ref, o_ref = refs
    else:
        hl_ref, hc_ref, mod_ref, o_ref, h_ref = refs
        _stream_tile(hl_ref, hc_ref, h_ref, i, n_lat_tiles)
    d = h_ref.shape[-1]
    for s in range(geom["segs_per_tile"]):
        row = _mod_row(i, s, **geom)
        gate = _mod_vec(mod_ref, row, k_gate, d)
        sl = slice(s * SEG, (s + 1) * SEG)
        y = jnp.dot(a_ref[sl, :], w_ref[...], preferred_element_type=F32)
        o_ref[sl, :] = h_ref[sl, :] + gate * y


def _proj_res(a, w, h, mod, geom, tm, k_gate=2):
    t, ka = a.shape
    two = isinstance(h, tuple)
    d = (h[0] if two else h).shape[1]
    n_lat_tiles = h[0].shape[0] // tm if two else None
    h_specs = _stream_specs(tm, d, n_lat_tiles) if two else [pl.BlockSpec((tm, d), lambda i: (i, 0))]
    return pl.pallas_call(
        functools.partial(_proj_res_kernel, geom=geom, k_gate=k_gate, n_lat_tiles=n_lat_tiles),
        grid=(t // tm,),
        in_specs=[pl.BlockSpec((tm, ka), lambda i: (i, 0)), _full(w)] + h_specs + [_full(mod)],
        out_specs=pl.BlockSpec((tm, d), lambda i: (i, 0)),
        out_shape=jax.ShapeDtypeStruct((t, d), F32),
        scratch_shapes=[pltpu.VMEM((tm, d), F32)] if two else [],
        compiler_params=_cparams(1),
        name="proj_res",
    )(a, w, *(h if two else (h,)), mod)


def _swiglu(xn, w13_ref, w2_ref, tf):
    f_total = w2_ref.shape[0]
    acc = None
    for f in range(f_total // tf):
        a = jnp.dot(xn, w13_ref[:, f * tf:(f + 1) * tf], preferred_element_type=F32)
        b = jnp.dot(xn, w13_ref[:, f_total + f * tf:f_total + (f + 1) * tf], preferred_element_type=F32)
        y = (a * jax.nn.sigmoid(a) * b).astype(BF16)
        part = jnp.dot(y, w2_ref[f * tf:(f + 1) * tf, :], preferred_element_type=F32)
        acc = part if acc is None else acc + part
    return acc


def _ffn_kernel(x_ref, mod_ref, g_ref, w13_ref, w2_ref, o_ref, xn_ref, *, geom, tf):
    i = pl.program_id(0)
    d = x_ref.shape[-1]
    _modulate_tile(x_ref, xn_ref, mod_ref, g_ref, i, 3, geom)
    acc = _swiglu(xn_ref[...], w13_ref, w2_ref, tf)
    for s in range(geom["segs_per_tile"]):
        row = _mod_row(i, s, **geom)
        gate2 = _mod_vec(mod_ref, row, 5, d)
        sl = slice(s * SEG, (s + 1) * SEG)
        o_ref[sl, :] = x_ref[sl, :] + gate2 * acc[sl, :]


def _resident(a):
    return pl.BlockSpec(a.shape, lambda *_: (0,) * a.ndim, pipeline_mode=pl.Buffered(1))


def _ffn(x, mod, g, w13, w2, geom, tm, tf):
    t, d = x.shape
    return pl.pallas_call(
        functools.partial(_ffn_kernel, geom=geom, tf=tf),
        grid=(t // tm,),
        in_specs=[pl.BlockSpec((tm, d), lambda i: (i, 0)), _full(mod), _full(g), _resident(w13), _resident(w2)],
        out_specs=pl.BlockSpec((tm, d), lambda i: (i, 0)),
        out_shape=jax.ShapeDtypeStruct((t, d), F32),
        scratch_shapes=[pltpu.VMEM((tm, d), BF16)],
        compiler_params=_cparams(1),
        name="ffn",
    )(x, mod, g, w13, w2)


def _router_kernel(x_ref, mod_ref, g_ref, wr_ref, info_ref, cnt_ref, xn_ref, tri_ref, run_ref, *, geom):
    i = pl.program_id(0)
    tm = x_ref.shape[0]

    @pl.when(i == 0)
    def _():
        r = lax.broadcasted_iota(jnp.int32, (tm, tm), 0)
        c = lax.broadcasted_iota(jnp.int32, (tm, tm), 1)
        tri_ref[...] = jnp.where(c < r, 1.0, 0.0).astype(tri_ref.dtype)
        run_ref[...] = jnp.zeros_like(run_ref)

    _modulate_tile(x_ref, xn_ref, mod_ref, g_ref, i, 3, geom)
    logits = jnp.dot(xn_ref[...], wr_ref[...], preferred_element_type=F32, precision=lax.Precision.HIGHEST)
    n_e = float(logits.shape[1])
    lane = lax.broadcasted_iota(jnp.int32, logits.shape, 1).astype(F32)
    m1 = jnp.max(logits, axis=1, keepdims=True)
    i1 = jnp.min(jnp.where(logits == m1, lane, n_e), axis=1, keepdims=True)
    rest = jnp.where(lane == i1, -jnp.inf, logits)
    m2 = jnp.max(rest, axis=1, keepdims=True)
    i2 = jnp.min(jnp.where(rest == m2, lane, n_e), axis=1, keepdims=True)
    e2 = jnp.exp(m2 - m1)
    w1 = 1.0 / (1.0 + e2)
    w2 = e2 / (1.0 + e2)
    onehot = jnp.where((lane == i1) | (lane == i2), 1.0, 0.0)
    before = jnp.dot(tri_ref[...], onehot.astype(tri_ref.dtype), preferred_element_type=F32) + run_ref[...]
    r1 = jnp.sum(jnp.where(lane == i1, before, 0.0), axis=1, keepdims=True)
    r2 = jnp.sum(jnp.where(lane == i2, before, 0.0), axis=1, keepdims=True)
    run_ref[...] += jnp.sum(onehot, axis=0, keepdims=True)
    cnt_ref[...] = run_ref[...]
    vals = (i1, i2, w1, w2, r1, r2)
    info = jnp.zeros_like(logits)
    for k, val in enumerate(vals):
        info = jnp.where(lane == float(k), val, info)
    info_ref[...] = info


def _router(x, mod, g, wr, geom, tm):
    t, d = x.shape
    n_e = wr.shape[1]
    assert n_e == INFO_COLS
    return pl.pallas_call(
        functools.partial(_router_kernel, geom=geom),
        grid=(t // tm,),
        in_specs=[pl.BlockSpec((tm, d), lambda i: (i, 0)), _full(mod), _full(g), _full(wr)],
        out_specs=[pl.BlockSpec((tm, INFO_COLS), lambda i: (i, 0)),
                   pl.BlockSpec((1, n_e), lambda i: (0, 0))],
        out_shape=[jax.ShapeDtypeStruct((t, INFO_COLS), F32), jax.ShapeDtypeStruct((1, n_e), F32)],
        scratch_shapes=[pltpu.VMEM((tm, d), F32), pltpu.VMEM((tm, tm), BF16), pltpu.VMEM((1, n_e), F32)],
        compiler_params=_cparams(1),
        name="router",
    )(x, mod, g, wr)


EXPERT_TILE_ROWS = 512
DMA_ISSUE_UNROLL = 8


def _row_copy(src_ref, src_row, dst_ref, dst_row, sem):
    return pltpu.make_async_copy(src_ref.at[pl.ds(src_row, 1)], dst_ref.at[pl.ds(dst_row, 1)], sem)


def _dispatch_kernel(slots_ref, ends_ref, x_ref, mod_ref, g_ref, xs_ref, xn_ref, sem, *, geom, n_tiles, rows):
    i = pl.program_id(0)
    tm = x_ref.shape[0]
    n_e = ends_ref.shape[0]

    @pl.when(i == 0)
    def _():
        xn_ref[0:rows, :] = jnp.zeros((rows, xn_ref.shape[1]), xn_ref.dtype)

        def zero_tile(row0):
            cp = pltpu.make_async_copy(xn_ref.at[pl.ds(0, rows)], xs_ref.at[pl.ds(row0, rows)], sem)
            cp.start()
            cp.wait()

        for e in range(n_e):
            start = ends_ref[e - 1] if e else 0
            last_tile = pl.multiple_of(ends_ref[e] - rows, rows)
            pl.when(ends_ref[e] > start)(functools.partial(zero_tile, last_tile))
        for j in range(n_tiles - n_e, n_tiles):
            pl.when(j * rows >= ends_ref[n_e - 1])(functools.partial(zero_tile, j * rows))

    _modulate_tile(x_ref, xn_ref, mod_ref, g_ref, i, 3, geom)

    def body(r, carry):
        for k in range(TOP_K):
            _row_copy(xn_ref, r, xs_ref, slots_ref[TOP_K * r + k], sem).start(priority=k)
        return carry

    lax.fori_loop(0, tm, body, 0, unroll=DMA_ISSUE_UNROLL)
    for k in range(TOP_K):
        pltpu.make_async_copy(xn_ref, xs_ref.at[pl.ds(0, tm)], sem).wait()


def _dispatch(slots, ends, x, mod, g, n_tiles, rows, geom, tm):
    t, d = x.shape
    assert rows <= tm
    return pl.pallas_call(
        functools.partial(_dispatch_kernel, geom=geom, n_tiles=n_tiles, rows=rows),
        grid=(t // tm,),
        in_specs=[pl.BlockSpec((TOP_K * tm,), lambda i: (i,), memory_space=pltpu.SMEM),
                  pl.BlockSpec(memory_space=pltpu.SMEM),
                  pl.BlockSpec((tm, d), lambda i: (i, 0)), _full(mod), _full(g)],
        out_specs=pl.BlockSpec(memory_space=pl.ANY),
        out_shape=jax.ShapeDtypeStruct((n_tiles * rows, d), F32),
        scratch_shapes=[pltpu.VMEM((tm, d), F32), pltpu.SemaphoreType.DMA(())],
        compiler_params=_cparams(1),
        name="moe_dispatch",
    )(slots, ends, x, mod, g)


def _experts_kernel(tile_expert_ref, n_used_ref, xs_ref, w13_ref, w2_ref, y_ref, *, tf):
    del tile_expert_ref
    i = pl.program_id(0)

    @pl.when(i < n_used_ref[0])
    def _():
        y_ref[...] = _swiglu(xs_ref[...].astype(BF16), w13_ref, w2_ref, tf)

    @pl.when(i >= n_used_ref[0])
    def _():
        y_ref[...] = jnp.zeros_like(y_ref)


def _experts(tile_expert, n_used, xs, w13, w2, rows, tf):
    p, d = xs.shape
    _, f_total, _ = w2.shape
    once = pl.Buffered(1)
    grid_spec = pltpu.PrefetchScalarGridSpec(
        num_scalar_prefetch=2,
        grid=(p // rows,),
        in_specs=[pl.BlockSpec((rows, d), lambda i, te, nu: (i, 0)),
                  pl.BlockSpec((None, d, 2 * f_total), lambda i, te, nu: (te[i], 0, 0), pipeline_mode=once),
                  pl.BlockSpec((None, f_total, d), lambda i, te, nu: (te[i], 0, 0), pipeline_mode=once)],
        out_specs=pl.BlockSpec((rows, d), lambda i, te, nu: (i, 0)),
    )
    return pl.pallas_call(
        functools.partial(_experts_kernel, tf=tf),
        grid_spec=grid_spec,
        out_shape=jax.ShapeDtypeStruct((p, d), F32),
        compiler_params=_cparams(1),
        name="moe_experts",
    )(tile_expert, n_used, xs, w13, w2)


def _combine_kernel(slots_ref, x_ref, info_ref, mod_ref, y_ref, *refs, geom):
    fg_ref = refs[0] if len(refs) == 4 else None
    o_ref, yb_ref, sem = refs[-3:]
    i = pl.program_id(0)
    tm, d = x_ref.shape

    def body(r, carry):
        for k in range(TOP_K):
            _row_copy(y_ref, slots_ref[TOP_K * r + k], yb_ref.at[k], r, sem).start(priority=k)
        return carry

    lax.fori_loop(0, tm, body, 0, unroll=DMA_ISSUE_UNROLL)
    for k in range(TOP_K):
        pltpu.make_async_copy(y_ref.at[pl.ds(0, tm)], yb_ref.at[k], sem).wait()

    lane = lax.broadcasted_iota(jnp.int32, (SEG, INFO_COLS), 1)
    for s in range(geom["segs_per_tile"]):
        row = _mod_row(i, s, **geom)
        gate2 = _mod_vec(mod_ref, row, 5, d)
        sl = slice(s * SEG, (s + 1) * SEG)
        info = info_ref[sl, :]
        w1 = jnp.sum(jnp.where(lane == 2, info, 0.0), axis=1, keepdims=True)
        w2 = jnp.sum(jnp.where(lane == 3, info, 0.0), axis=1, keepdims=True)
        out = x_ref[sl, :] + gate2 * (w1 * yb_ref[0, sl, :] + w2 * yb_ref[1, sl, :])
        o_ref[sl, :] = out if fg_ref is None else _rms(out) * fg_ref[...]


def _combine(slots, x, info, mod, y, geom, tm, final=None):
    t, d = x.shape
    n_rows = final[1] if final else t
    extra = [final[0]] if final else []
    return pl.pallas_call(
        functools.partial(_combine_kernel, geom=geom),
        grid=(n_rows // tm,),
        in_specs=[pl.BlockSpec((TOP_K * tm,), lambda i: (i,), memory_space=pltpu.SMEM),
                  pl.BlockSpec((tm, d), lambda i: (i, 0)),
                  pl.BlockSpec((tm, INFO_COLS), lambda i: (i, 0)), _full(mod),
                  pl.BlockSpec(memory_space=pl.ANY)] + [_full(a) for a in extra],
        out_specs=pl.BlockSpec((tm, d), lambda i: (i, 0)),
        out_shape=jax.ShapeDtypeStruct((n_rows, d), F32),
        scratch_shapes=[pltpu.VMEM((TOP_K, tm, d), F32), pltpu.SemaphoreType.DMA(())],
        compiler_params=_cparams(1),
        name="moe_combine",
    )(slots, x, info, mod, y, *extra)


def _moe(x, mod, g, wr, w13, w2, geom, tm, tf, final=None):
    t, d = x.shape
    n_e = wr.shape[1]
    info, counts = _router(x, mod, g, wr, geom, tm)
    rows = EXPERT_TILE_ROWS
    counts = counts[0].astype(jnp.int32)
    padded = (counts + rows - 1) // rows * rows
    ends = jnp.cumsum(padded)
    starts = ends - padded
    n_tiles = TOP_K * t // rows + n_e
    tile_start = jnp.arange(n_tiles, dtype=jnp.int32) * rows
    tile_expert = jnp.minimum(jnp.sum((ends[None, :] <= tile_start[:, None]).astype(jnp.int32), axis=1), n_e - 1)
    n_used = (ends[-1:] // rows).astype(jnp.int32)
    experts = info[:, :TOP_K].astype(jnp.int32)
    ranks = info[:, 4:4 + TOP_K].astype(jnp.int32)
    slots = (starts[experts] + ranks).reshape(-1)
    xs = _dispatch(slots, ends, x, mod, g, n_tiles, rows, geom, tm)
    y = _experts(tile_expert, n_used, xs, w13, w2, rows, tf)
    return _combine(slots, x, info, mod, y, geom, tm, final)


def _proj_b_kernel(x_ref, mod_ref, g_ref, wd_ref, qg_ref, kvg_ref, wq_ref, wk_ref, wv_ref,
                   cos_ref, sa_ref, sb_ref, q_ref, k_ref, v_ref, xn_ref, *, geom, scale):
    i = pl.program_id(0)
    _modulate_tile(x_ref, xn_ref, mod_ref, g_ref, i, 0, geom)
    xn = xn_ref[...]
    cos, sa, sb = cos_ref[...], sa_ref[...], sb_ref[...]
    dq = jnp.dot(xn, wd_ref[:, :B_Q_LORA], preferred_element_type=F32)
    dkv = jnp.dot(xn, wd_ref[:, B_Q_LORA:B_Q_LORA + B_KV_LORA], preferred_element_type=F32)
    kr = jnp.dot(xn, wd_ref[:, B_Q_LORA + B_KV_LORA:], preferred_element_type=F32)
    dqn = (_rms(dq) * qg_ref[...]).astype(BF16)
    dkvn = (_rms(dkv) * kvg_ref[...]).astype(BF16)
    kr = _rope(kr, cos, sa, sb, B_ROPE_DIM // 4).astype(k_ref.dtype)
    for h in range(B_HEADS):
        qh = jnp.dot(dqn, wq_ref[:, h * MXU_N:(h + 1) * MXU_N], preferred_element_type=F32)
        q_ref[:, h * MXU_N:h * MXU_N + LANES] = (qh[:, :LANES] * scale).astype(q_ref.dtype)
        q_ref[:, h * MXU_N + LANES:(h + 1) * MXU_N] = (
            _rope(qh[:, LANES:], cos, sa, sb, B_ROPE_DIM // 4) * scale).astype(q_ref.dtype)
        k_ref[:, h * MXU_N + LANES:(h + 1) * MXU_N] = kr
    for pair in range(B_HEADS // 2):
        sl = slice(pair * MXU_N, (pair + 1) * MXU_N)
        kn = jnp.dot(dkvn, wk_ref[:, sl], preferred_element_type=F32)
        for half in range(2):
            h = 2 * pair + half
            k_ref[:, h * MXU_N:h * MXU_N + LANES] = kn[:, half * LANES:(half + 1) * LANES].astype(k_ref.dtype)
        v_ref[:, sl] = jnp.dot(dkvn, wv_ref[:, sl], preferred_element_type=F32).astype(v_ref.dtype)


def _proj_b(x, mod, g, wd, qg, kvg, wq, wk, wv, tabs, geom, tm):
    t, d = x.shape
    rows = lambda width: pl.BlockSpec((tm, width), lambda i: (i, 0))
    nq, nv = B_HEADS * MXU_N, B_HEADS * B_V_DIM
    return pl.pallas_call(
        functools.partial(_proj_b_kernel, geom=geom, scale=(B_NOPE_DIM + B_ROPE_DIM) ** -0.5 * LOG2E),
        grid=(t // tm,),
        in_specs=[rows(d), _full(mod), _full(g), _full(wd), _full(qg), _full(kvg), _full(wq), _full(wk),
                  _full(wv), rows(LANES), rows(LANES), rows(LANES)],
        out_specs=[rows(nq), rows(nq), rows(nv)],
        out_shape=[jax.ShapeDtypeStruct((t, nq), BF16), jax.ShapeDtypeStruct((t, nq), BF16),
                   jax.ShapeDtypeStruct((t, nv), BF16)],
        scratch_shapes=[pltpu.VMEM((tm, d), BF16)],
        compiler_params=_cparams(1),
        name="proj_b",
    )(x, mod, g, wd, qg, kvg, wq, wk, wv, *tabs)


def _gelu(x):
    return 0.5 * x * (1.0 + lax.erf(x * (2.0 ** -0.5)))


def _sgu_kernel(x_ref, mod_ref, g_ref, win_ref, lng_ref, lnb_ref, ws_ref, bs_ref, wout_ref, o_ref,
                xn_ref, u_ref, v_ref, gated_ref, *, geom):
    i = pl.program_id(0)
    tm, d = x_ref.shape
    cw = u_ref.shape[1]
    _modulate_tile(x_ref, xn_ref, mod_ref, g_ref, i, 0, geom)
    xn = xn_ref[...]
    for j in range(cw // MXU_N):
        sl = slice(j * MXU_N, (j + 1) * MXU_N)
        u_ref[:, sl] = _gelu(jnp.dot(xn, win_ref[:, sl], preferred_element_type=F32))
        v_ref[:, sl] = _gelu(jnp.dot(xn, win_ref[:, cw + j * MXU_N:cw + (j + 1) * MXU_N],
                                     preferred_element_type=F32))
    lng, lnb = lng_ref[...], lnb_ref[...]
    gw = cw // C_GROUPS
    for c in range(tm // C_CHUNK):
        rs = slice(c * C_CHUNK, (c + 1) * C_CHUNK)
        v = v_ref[rs, :]
        mu = _row_sum(v) * (1.0 / cw)
        vc = v - mu
        var = _row_sum(vc * vc) * (1.0 / cw)
        vn = (vc * lax.rsqrt(var + EPS) * lng + lnb).astype(BF16)
        for gi in range(C_GROUPS):
            cs = slice(gi * gw, (gi + 1) * gw)
            mixed = jnp.dot(ws_ref[gi], vn[:, cs], preferred_element_type=F32) + bs_ref[:, cs]
            gated_ref[rs, cs] = (u_ref[rs, cs] * mixed).astype(gated_ref.dtype)
    for s in range(geom["segs_per_tile"]):
        row = _mod_row(i, s, **geom)
        gate = _mod_vec(mod_ref, row, 2, d)
        sl = slice(s * SEG, (s + 1) * SEG)
        y = jnp.dot(gated_ref[sl, :], wout_ref[...], preferred_element_type=F32)
        o_ref[sl, :] = x_ref[sl, :] + gate * y


def _sgu(x, mod, g, win, lng, lnb, ws, bs, wout, geom, tm):
    t, d = x.shape
    cw = wout.shape[0]
    return pl.pallas_call(
        functools.partial(_sgu_kernel, geom=geom),
        grid=(t // tm,),
        in_specs=[pl.BlockSpec((tm, d), lambda i: (i, 0)), _full(mod), _full(g), _full(win), _full(lng),
                  _full(lnb), _full(ws), _full(bs), _full(wout)],
        out_specs=pl.BlockSpec((tm, d), lambda i: (i, 0)),
        out_shape=jax.ShapeDtypeStruct((t, d), F32),
        scratch_shapes=[pltpu.VMEM((tm, d), BF16), pltpu.VMEM((tm, cw), F32), pltpu.VMEM((tm, cw), F32),
                        pltpu.VMEM((tm, cw), BF16)],
        compiler_params=_cparams(1),
        name="sgu",
    )(x, mod, g, win, lng, lnb, ws, bs, wout)


def _proj_d_kernel(x_ref, mod_ref, g_ref, w_ref, cos_ref, sa_ref, sb_ref, q_ref, k_ref, v_ref, xn_ref,
                   *, geom, scale):
    i = pl.program_id(0)
    _modulate_tile(x_ref, xn_ref, mod_ref, g_ref, i, 0, geom)
    xn = xn_ref[...]
    cos, sa, sb = cos_ref[...], sa_ref[...], sb_ref[...]
    nq = q_ref.shape[1]
    for pair in range(nq // MXU_N):
        y2 = jnp.dot(xn, w_ref[:, pair * MXU_N:(pair + 1) * MXU_N], preferred_element_type=F32)
        for half in range(2):
            sl = slice(half * LANES, (half + 1) * LANES)
            y = _rope(y2[:, sl], cos, sa, sb, D_HEAD_DIM // 4) * scale
            q_ref[:, pair * MXU_N + half * LANES:pair * MXU_N + (half + 1) * LANES] = y.astype(q_ref.dtype)
    kv = jnp.dot(xn, w_ref[:, nq:nq + MXU_N], preferred_element_type=F32)
    k_ref[...] = _rope(kv[:, :LANES], cos, sa, sb, D_HEAD_DIM // 4).astype(k_ref.dtype)
    v_ref[...] = kv[:, LANES:].astype(v_ref.dtype)


def _proj_d(x, mod, g, w, tabs, geom, tm):
    t, d = x.shape
    nq = D_HEADS * D_HEAD_DIM
    rows = lambda width: pl.BlockSpec((tm, width), lambda i: (i, 0))
    return pl.pallas_call(
        functools.partial(_proj_d_kernel, geom=geom, scale=D_HEAD_DIM ** -0.5 * LOG2E),
        grid=(t // tm,),
        in_specs=[rows(d), _full(mod), _full(g), _full(w), rows(LANES), rows(LANES), rows(LANES)],
        out_specs=[rows(nq), rows(LANES), rows(LANES)],
        out_shape=[jax.ShapeDtypeStruct((t, nq), BF16), jax.ShapeDtypeStruct((t, LANES), BF16),
                   jax.ShapeDtypeStruct((t, LANES), BF16)],
        scratch_shapes=[pltpu.VMEM((tm, d), BF16)],
        compiler_params=_cparams(1),
        name="proj_d",
    )(x, mod, g, w, *tabs)


def _win_kernel(sink_ref, q_ref, kc_ref, km_ref, k0_ref, kp_ref, vc_ref, vm_ref, v0_ref, vp_ref, o_ref,
                *, lat_blocks, blocks_per_sample):
    j = pl.program_id(0)
    blk = q_ref.shape[0]
    ctx = kc_ref.shape[0]
    kk = jnp.concatenate([kc_ref[...], km_ref[...], k0_ref[...], kp_ref[...]], axis=0)
    vv = jnp.concatenate([vc_ref[...], vm_ref[...], v0_ref[...], vp_ref[...]], axis=0)
    n_keys = kk.shape[0]
    n = lax.rem(j, jnp.int32(blocks_per_sample))
    qpos = lax.broadcasted_iota(jnp.int32, (blk, n_keys), 0)
    col = lax.broadcasted_iota(jnp.int32, (blk, n_keys), 1)
    rel = col - ctx - blk - qpos
    kblock = n - 1 + lax.shift_right_arithmetic(col - ctx, int(math.log2(blk)))
    far = jnp.where(j < lat_blocks, 0, 2 * D_WINDOW + 2)
    in_window = (jnp.abs(rel) + far <= D_WINDOW) & (kblock >= 0) & (kblock < blocks_per_sample)
    mask = (col < ctx) | in_window
    group = D_HEADS // D_KV_HEADS
    bias = jnp.concatenate([jnp.where(mask, 0.0, MASK_VALUE)] * group, axis=0)
    for hk in range(D_KV_HEADS):
        heads = range(hk * group, (hk + 1) * group)
        q = jnp.concatenate([q_ref[:, h * D_HEAD_DIM:(h + 1) * D_HEAD_DIM] for h in heads], axis=0)
        sink = jnp.concatenate([jnp.full((blk, 1), sink_ref[h] * LOG2E, F32) for h in heads], axis=0)
        k = kk[:, hk * D_HEAD_DIM:(hk + 1) * D_HEAD_DIM]
        v = vv[:, hk * D_HEAD_DIM:(hk + 1) * D_HEAD_DIM]
        s = lax.dot_general(q, k, (((1,), (1,)), ((), ())), preferred_element_type=F32) + bias
        m = jnp.maximum(_row_max(s), sink)
        e = jnp.exp2(s - m)
        denom = _row_sum(e) + jnp.exp2(sink - m)
        o = jnp.dot(e.astype(v.dtype), v, preferred_element_type=F32) / denom
        for n_h, h in enumerate(heads):
            o_ref[:, h * D_HEAD_DIM:(h + 1) * D_HEAD_DIM] = o[n_h * blk:(n_h + 1) * blk, :].astype(o_ref.dtype)


def _win_attn(sinks, q, k, v, *, n_batch, lat, ctx, blk=128):
    t, nq = q.shape
    bps = lat // blk
    lat_blocks = n_batch * bps
    ctx_bps = ctx // blk
    ctx_blk0 = n_batch * lat // ctx

    def sample(j):
        return jnp.where(j < lat_blocks, j // bps, (j - lat_blocks) // ctx_bps)

    def neighbour(j, off):
        b = sample(j)
        n = jnp.where(j < lat_blocks, j - b * bps, 0)
        return b * bps + jnp.clip(n + off, 0, bps - 1)

    kv_specs = [pl.BlockSpec((ctx, LANES), lambda j: (ctx_blk0 + sample(j), 0)),
                pl.BlockSpec((blk, LANES), lambda j: (neighbour(j, -1), 0)),
                pl.BlockSpec((blk, LANES), lambda j: (neighbour(j, 0), 0)),
                pl.BlockSpec((blk, LANES), lambda j: (neighbour(j, 1), 0))]
    return pl.pallas_call(
        functools.partial(_win_kernel, lat_blocks=lat_blocks, blocks_per_sample=bps),
        grid=(t // blk,),
        in_specs=[pl.BlockSpec(memory_space=pltpu.SMEM),
                  pl.BlockSpec((blk, nq), lambda j: (j, 0))] + kv_specs + kv_specs,
        out_specs=pl.BlockSpec((blk, nq), lambda j: (j, 0)),
        out_shape=jax.ShapeDtypeStruct((t, nq), BF16),
        compiler_params=_cparams(1),
        name="win_attn",
    )(sinks, q, k, k, k, k, v, v, v, v)


def _rope_tables(n_batch, lat, ctx, dim, tiled):
    quarter = dim // 4
    pos = jnp.arange(lat, dtype=jnp.int32)
    rows, cols = pos // GRID_W, pos % GRID_W
    inv_freq = ROPE_THETA ** (-jnp.arange(quarter, dtype=F32) / quarter)
    ang_r = rows.astype(F32)[:, None] * inv_freq
    ang_c = cols.astype(F32)[:, None] * inv_freq
    ang = jnp.concatenate([ang_r, ang_r, ang_c, ang_c], axis=-1)
    cos, sin = jnp.cos(ang), jnp.sin(ang)
    first = (jnp.arange(dim) % (2 * quarter)) < quarter
    out = []
    for tab, fill in ((cos, 1.0), (jnp.where(first, -sin, 0.0), 0.0), (jnp.where(first, 0.0, sin), 0.0)):
        if tiled:
            tab = jnp.tile(tab, (1, LANES // dim))
        else:
            tab = jnp.concatenate([tab, jnp.full((lat, LANES - dim), fill, F32)], axis=1)
        out.append(jnp.concatenate([jnp.tile(tab, (n_batch, 1)), jnp.full((n_batch * ctx, LANES), fill, F32)],
                                   axis=0))
    return tuple(out)


def _pick_tf(f, cap):
    return max(tf for tf in range(LANES, cap + 1, LANES) if f % tf == 0)


def kernel(x, c, ctx, c_ctx, ada_w, ada_b, norm_mix, norm_ffn, final_norm, a_wqkv, a_q_norm, a_k_norm, a_wo, b_w_down, b_q_lora_norm, b_kv_lora_norm, b_w_uq, b_w_ukv, b_wo, c_w_in, c_ln_g, c_ln_b, c_w_spatial, c_b_spatial, c_w_out, d_wqkv, d_sinks, d_wo, ffn_w13, ffn_w2, moe_router, moe_w13, moe_w2):
    n_batch, lat, d = x.shape
    n_ctx = ctx.shape[1]
    depth = ada_w.shape[0]
    assert n_ctx == SEG and lat % 1024 == 0 and n_batch < MOD_ROWS and depth == 4
    ctx_segs = n_batch * n_ctx // SEG
    segs_per_tile = max(k for k in (4, 2, 1) if ctx_segs % k == 0)
    tm = segs_per_tile * SEG
    geom = dict(segs_per_tile=segs_per_tile, lat_segs=n_batch * lat // SEG, segs_per_sample=lat // SEG,
                n_batch=n_batch)
    half_segs = min(2, segs_per_tile)
    geom_half, tm_half = dict(geom, segs_per_tile=half_segs), half_segs * SEG

    cvec = jnp.concatenate([c, c_ctx[None, :], jnp.zeros((MOD_ROWS - n_batch - 1, d), F32)], axis=0)
    mod = _ada_mod(cvec, ada_w, ada_b)
    h = (x.reshape(n_batch * lat, d), ctx.reshape(n_batch * n_ctx, d))

    tabs_a = _rope_tables(n_batch, lat, n_ctx, A_HEAD_DIM, True)
    tabs_d = _rope_tables(n_batch, lat, n_ctx, D_HEAD_DIM, True)
    tabs_b = _rope_tables(n_batch, lat, n_ctx, B_ROPE_DIM, False)
    row = lambda v: v.reshape(1, -1)
    dense_tf = _pick_tf(ffn_w13.shape[-1] // 2, 256)
    moe_tf = _pick_tf(moe_w13.shape[-1] // 2, 512)
    attn = functools.partial(_flash, n_batch=n_batch, lat=lat, ctx=n_ctx)

    qkv = _proj_a(*h, mod[0], row(norm_mix[0]), a_wqkv[0].astype(BF16), row(a_q_norm[0]), row(a_k_norm[0]),
                  tabs_a, geom, tm)
    grp = A_HEADS // A_KV_HEADS
    o = attn(qkv, qkv, qkv, n_groups=A_KV_HEADS, heads=grp, dq=A_HEAD_DIM, dv=A_HEAD_DIM,
             k_col=lambda g: A_HEADS + g, v_col=lambda g: A_HEADS + A_KV_HEADS + g, tq=256)
    h = _proj_res(o, a_wo[0].astype(BF16), h, mod[0], geom, tm)
    h = _ffn(h, mod[0], row(norm_ffn[0]), ffn_w13[0].astype(BF16), ffn_w2[0].astype(BF16), geom_half, tm_half,
             dense_tf)

    wd = b_w_down[0]
    wd = jnp.concatenate([wd, jnp.zeros((d, LANES - B_ROPE_DIM), F32)], axis=1).astype(BF16)
    wq = b_w_uq[0].reshape(B_Q_LORA, B_HEADS, B_NOPE_DIM + B_ROPE_DIM)
    wq = jnp.concatenate([wq, jnp.zeros((B_Q_LORA, B_HEADS, MXU_N - B_NOPE_DIM - B_ROPE_DIM), F32)], axis=2)
    wq = wq.reshape(B_Q_LORA, B_HEADS * MXU_N).astype(BF16)
    wkv = b_w_ukv[0].reshape(B_KV_LORA, B_HEADS, B_NOPE_DIM + B_V_DIM)
    wk = wkv[:, :, :B_NOPE_DIM].reshape(B_KV_LORA, B_HEADS * B_NOPE_DIM).astype(BF16)
    wv = wkv[:, :, B_NOPE_DIM:].reshape(B_KV_LORA, B_HEADS * B_V_DIM).astype(BF16)
    qc, kc, vb = _proj_b(h, mod[1], row(norm_mix[1]), wd, row(b_q_lora_norm[0]), row(b_kv_lora_norm[0]),
                         wq, wk, wv, tabs_b, geom, tm)
    o = attn(qc, kc, vb, n_groups=B_HEADS, heads=1, dq=MXU_N, dv=B_V_DIM,
             k_col=lambda g: g, v_col=lambda g: g, tq=1024)
    h = _proj_res(o, b_wo[0].astype(BF16), h, mod[1], geom, tm)
    h = _moe(h, mod[1], row(norm_ffn[1]), moe_router[0], moe_w13[0].astype(BF16), moe_w2[0].astype(BF16),
             geom, tm, moe_tf)

    bs = jnp.repeat(c_b_spatial[0].T, c_w_out.shape[1] // C_GROUPS, axis=1)
    h = _sgu(h, mod[2], row(norm_mix[2]), c_w_in[0].astype(BF16), row(c_ln_g[0]), row(c_ln_b[0]),
             c_w_spatial[0].astype(BF16), bs, c_w_out[0].astype(BF16), geom_half, tm_half)
    h = _ffn(h, mod[2], row(norm_ffn[2]), ffn_w13[1].astype(BF16), ffn_w2[1].astype(BF16), geom_half, tm_half,
             dense_tf)

    q3, k3, v3 = _proj_d(h, mod[3], row(norm_mix[3]), d_wqkv[0].astype(BF16), tabs_d, geom, tm)
    o = _win_attn(d_sinks[0], q3, k3, v3, n_batch=n_batch, lat=lat, ctx=n_ctx)
    h = _proj_res(o, d_wo[0].astype(BF16), h, mod[3], geom, tm)
    out = _moe(h, mod[3], row(norm_ffn[3]), moe_router[1], moe_w13[1].astype(BF16), moe_w2[1].astype(BF16),
               geom, tm, moe_tf, final=(row(final_norm), n_batch * lat))
    return out.reshape(n_batch, lat, d)
```

```python
import functools
import math

import jax
import jax.numpy as jnp
from jax import lax
from jax.experimental import pallas as pl
from jax.experimental.pallas import tpu as pltpu

F32 = jnp.float32
BF16 = jnp.bfloat16

EPS = 1e-6
ROPE_THETA = 10000.0
GRID_W = 64
MASK_VALUE = -1e30
LOG2E = math.log2(math.e)

A_HEADS, A_KV_HEADS, A_HEAD_DIM = 8, 2, 128
B_HEADS, B_Q_LORA, B_KV_LORA, B_NOPE_DIM, B_ROPE_DIM, B_V_DIM = 8, 384, 256, 128, 64, 128
C_GROUPS, C_CHUNK = 8, 128
D_HEADS, D_KV_HEADS, D_HEAD_DIM, D_WINDOW = 16, 2, 64, 128
TOP_K = 2

SEG = 256
LANES = 128
MXU_N = 256
VMEM_LIMIT = 56 * 1024 * 1024

MOD_ROWS = 8
INFO_COLS = 8


def _cparams(n_axes):
    return pltpu.CompilerParams(dimension_semantics=("arbitrary",) * n_axes,
                                vmem_limit_bytes=VMEM_LIMIT)


def _full(a):
    return pl.BlockSpec(a.shape, lambda *_: (0,) * a.ndim)


def _mod_row(tile, s, segs_per_tile, lat_segs, segs_per_sample, n_batch):
    g = tile * segs_per_tile + s
    return jnp.where(g < lat_segs, lax.div(g, jnp.int32(segs_per_sample)), n_batch)


def _mod_vec(mod_ref, row, k, d):
    return mod_ref[pl.ds(row, 1), k * d:(k + 1) * d]


def _row_sum(x):
    return jnp.sum(x, axis=-1, keepdims=True)


def _row_max(x):
    return jnp.max(x, axis=-1, keepdims=True)


def _rms(x):
    return x * lax.rsqrt(jnp.mean(x * x, axis=-1, keepdims=True) + EPS)


def _modulate_tile(x_ref, xn_ref, mod_ref, g_ref, tile, k_shift, geom):
    d = x_ref.shape[-1]
    g = g_ref[...]
    for s in range(geom["segs_per_tile"]):
        row = _mod_row(tile, s, **geom)
        shift = _mod_vec(mod_ref, row, k_shift, d)
        scale = _mod_vec(mod_ref, row, k_shift + 1, d)
        x = x_ref[s * SEG:(s + 1) * SEG, :]
        y = _rms(x) * g * (1.0 + scale) + shift
        xn_ref[s * SEG:(s + 1) * SEG, :] = y.astype(xn_ref.dtype)


def _rope(t, cos, sin_a, sin_b, shift):
    return (t * cos + pltpu.roll(t, LANES - shift, axis=1) * sin_a
            + pltpu.roll(t, shift, axis=1) * sin_b)


def _ada_kernel(c_ref, w_ref, b_ref, o_ref):
    c = c_ref[...]
    a = c * jax.nn.sigmoid(c)
    o_ref[0] = jnp.dot(a, w_ref[0], preferred_element_type=F32,
                       precision=lax.Precision.HIGHEST) + b_ref[0]


def _ada_mod(cvec, ada_w, ada_b):
    depth, d, m = ada_w.shape
    tn = m // 4
    return pl.pallas_call(
        _ada_kernel,
        grid=(depth, m // tn),
        in_specs=[pl.BlockSpec((MOD_ROWS, d), lambda l, j: (0, 0)),
                  pl.BlockSpec((1, d, tn), lambda l, j: (l, 0, j)),
                  pl.BlockSpec((1, 1, tn), lambda l, j: (l, 0, j))],
        out_specs=pl.BlockSpec((1, MOD_ROWS, tn), lambda l, j: (l, 0, j)),
        out_shape=jax.ShapeDtypeStruct((depth, MOD_ROWS, m), F32),
        compiler_params=_cparams(2),
        name="ada_mod",
    )(cvec, ada_w, ada_b.reshape(depth, 1, m))


def _stream_tile(xl_ref, xc_ref, buf_ref, i, n_lat_tiles):
    @pl.when(i < n_lat_tiles)
    def _():
        buf_ref[...] = xl_ref[...]

    @pl.when(i >= n_lat_tiles)
    def _():
        buf_ref[...] = xc_ref[...]


def _stream_specs(tm, d, n_lat_tiles):
    return [pl.BlockSpec((tm, d), lambda i: (jnp.minimum(i, n_lat_tiles - 1), 0)),
            pl.BlockSpec((tm, d), lambda i: (jnp.maximum(i - n_lat_tiles, 0), 0))]


def _proj_a_kernel(xl_ref, xc_ref, mod_ref, g_ref, w_ref, qn_ref, kn_ref, cos_ref, sa_ref, sb_ref, o_ref,
                   x_ref, xn_ref, *, geom, scale, n_lat_tiles):
    i = pl.program_id(0)
    _stream_tile(xl_ref, xc_ref, x_ref, i, n_lat_tiles)
    _modulate_tile(x_ref, xn_ref, mod_ref, g_ref, i, 0, geom)
    xn = xn_ref[...]
    cos, sa, sb = cos_ref[...], sa_ref[...], sb_ref[...]
    qn, kn = qn_ref[...], kn_ref[...]
    n_q, n_k = A_HEADS, A_KV_HEADS
    n_heads = n_q + 2 * n_k
    for pair in range(n_heads // 2):
        y2 = jnp.dot(xn, w_ref[:, pair * MXU_N:(pair + 1) * MXU_N], preferred_element_type=F32)
        for half in range(2):
            h = 2 * pair + half
            y = y2[:, half * LANES:(half + 1) * LANES]
            if h < n_q:
                y = _rope(_rms(y) * qn, cos, sa, sb, A_HEAD_DIM // 4) * scale
            elif h < n_q + n_k:
                y = _rope(_rms(y) * kn, cos, sa, sb, A_HEAD_DIM // 4)
            o_ref[:, h * LANES:(h + 1) * LANES] = y.astype(o_ref.dtype)


def _proj_a(xl, xc, mod, g, w, qn, kn, tabs, geom, tm):
    d = xl.shape[1]
    t = xl.shape[0] + xc.shape[0]
    n = w.shape[1]
    n_lat_tiles = xl.shape[0] // tm
    rows = lambda width: pl.BlockSpec((tm, width), lambda i: (i, 0))
    return pl.pallas_call(
        functools.partial(_proj_a_kernel, geom=geom, scale=A_HEAD_DIM ** -0.5 * LOG2E, n_lat_tiles=n_lat_tiles),
        grid=(t // tm,),
        in_specs=_stream_specs(tm, d, n_lat_tiles) + [_full(mod), _full(g), _full(w), _full(qn), _full(kn),
                                                       rows(LANES), rows(LANES), rows(LANES)],
        out_specs=rows(n),
        out_shape=jax.ShapeDtypeStruct((t, n), BF16),
        scratch_shapes=[pltpu.VMEM((tm, d), F32), pltpu.VMEM((tm, d), BF16)],
        compiler_params=_cparams(1),
        name="proj_a",
    )(xl, xc, mod, g, w, qn, kn, *tabs)


FLASH_UNROLL = 8


def _flash_kernel(*refs, heads, dq, dv, tk, n_lat_chunks):
    if n_lat_chunks:
        _, q_ref, kc_ref, vc_ref, kl_ref, vl_ref, o_ref = refs
    else:
        _, q_ref, kc_ref, vc_ref, o_ref = refs
    tq = q_ref.shape[0]
    if heads > 1:
        q = jnp.concatenate([q_ref[:, g * dq:(g + 1) * dq] for g in range(heads)], axis=0)
    else:
        q = q_ref[...]
    rows = heads * tq

    def scores(k):
        return lax.dot_general(q, k, (((1,), (1,)), ((), ())), preferred_element_type=F32)

    def update(s, v, m, acc):
        m_new = jnp.maximum(m, _row_max(s))
        alpha = jnp.exp2(m - m_new)
        p = jnp.exp2((s - m_new).astype(v.dtype))
        v_ones = jnp.concatenate([v, jnp.ones_like(v)], axis=1)
        acc = alpha * acc + jnp.dot(p, v_ones, preferred_element_type=F32)
        return m_new, acc

    m0 = jnp.full((rows, 1), MASK_VALUE, F32)
    acc0 = jnp.zeros((rows, 2 * dv), F32)
    m, acc = update(scores(kc_ref[...]), vc_ref[...], m0, acc0)
    if n_lat_chunks:
        def body(c, carry):
            off = pl.multiple_of(c * tk, tk)
            return update(scores(kl_ref[pl.ds(off, tk), :]), vl_ref[pl.ds(off, tk), :], *carry)

        m, acc = lax.fori_loop(0, n_lat_chunks, body, (m, acc), unroll=FLASH_UNROLL)
    o = acc[:, :dv] / acc[:, dv:]
    for g in range(heads):
        o_ref[:, g * dv:(g + 1) * dv] = o[g * tq:(g + 1) * tq, :].astype(o_ref.dtype)


def _flash(q, k, v, *, n_batch, lat, ctx, n_groups, heads, dq, dv, k_col, v_col, tq, tk=512):
    t = q.shape[0]
    nq = lat // tq
    ctx_blk0 = n_batch * lat // ctx
    kern = functools.partial(_flash_kernel, heads=heads, dq=dq, dv=dv, tk=tk)
    out_shape = jax.ShapeDtypeStruct((t, n_groups * heads * dv), BF16)
    o = pl.pallas_call(
        functools.partial(kern, n_lat_chunks=lat // tk),
        grid=(n_batch, n_groups, nq),
        in_specs=[pl.BlockSpec(memory_space=pl.ANY),
                  pl.BlockSpec((tq, heads * dq), lambda b, g, i: (b * nq + i, g)),
                  pl.BlockSpec((ctx, dq), lambda b, g, i: (ctx_blk0 + b, k_col(g))),
                  pl.BlockSpec((ctx, dv), lambda b, g, i: (ctx_blk0 + b, v_col(g))),
                  pl.BlockSpec((lat, dq), lambda b, g, i: (b, k_col(g))),
                  pl.BlockSpec((lat, dv), lambda b, g, i: (b, v_col(g)))],
        out_specs=pl.BlockSpec((tq, heads * dv), lambda b, g, i: (b * nq + i, g)),
        out_shape=out_shape,
        input_output_aliases={0: 0},
        compiler_params=_cparams(3),
        name="flash_lat",
    )(jnp.zeros(out_shape.shape, out_shape.dtype), q, k, v, k, v)
    return pl.pallas_call(
        functools.partial(kern, n_lat_chunks=0),
        grid=(n_batch, n_groups),
        in_specs=[pl.BlockSpec(memory_space=pl.ANY),
                  pl.BlockSpec((ctx, heads * dq), lambda b, g: (ctx_blk0 + b, g)),
                  pl.BlockSpec((ctx, dq), lambda b, g: (ctx_blk0 + b, k_col(g))),
                  pl.BlockSpec((ctx, dv), lambda b, g: (ctx_blk0 + b, v_col(g)))],
        out_specs=pl.BlockSpec((ctx, heads * dv), lambda b, g: (ctx_blk0 + b, g)),
        out_shape=out_shape,
        input_output_aliases={0: 0},
        compiler_params=_cparams(2),
        name="flash_ctx",
    )(o, q, k, v)


def _proj_res_kernel(a_ref, w_ref, h_ref, mod_ref, o_ref, *, geom):
    i = pl.program_id(0)
    d = h_ref.shape[-1]
    for s in range(geom["segs_per_tile"]):
        row = _mod_row(i, s, **geom)
        gate = _mod_vec(mod_ref, row, 2, d)
        sl = slice(s * SEG, (s + 1) * SEG)
        y = jnp.dot(a_ref[sl, :], w_ref[...], preferred_element_type=F32)
        o_ref[sl, :] = h_ref[sl, :] + gate * y


def _proj_res(a, w, h, mod, geom, tm):
    t, ka = a.shape
    d = h.shape[1]
    return pl.pallas_call(
        functools.partial(_proj_res_kernel, geom=geom),
        grid=(t // tm,),
        in_specs=[pl.BlockSpec((tm, ka), lambda i: (i, 0)), _full(w), pl.BlockSpec((tm, d), lambda i: (i, 0)),
                  _full(mod)],
        out_specs=pl.BlockSpec((tm, d), lambda i: (i, 0)),
        out_shape=jax.ShapeDtypeStruct((t, d), F32),
        compiler_params=_cparams(1),
        name="proj_res",
    )(a, w, h, mod)


def _swiglu(xn, w13_ref, w2_ref, tf):
    f_total = w2_ref.shape[0]
    acc = None
    for f in range(f_total // tf):
        a = jnp.dot(xn, w13_ref[:, f * tf:(f + 1) * tf], preferred_element_type=F32)
        b = jnp.dot(xn, w13_ref[:, f_total + f * tf:f_total + (f + 1) * tf], preferred_element_type=F32)
        y = (a * jax.nn.sigmoid(a) * b).astype(BF16)
        part = jnp.dot(y, w2_ref[f * tf:(f + 1) * tf, :], preferred_element_type=F32)
        acc = part if acc is None else acc + part
    return acc


def _mix_ffn_kernel(a_ref, wo_ref, *refs, geom, tf, n_lat_tiles):
    i = pl.program_id(0)
    if n_lat_tiles is None:
        h_ref, mod_ref, g_ref, w13_ref, w2_ref, o_ref, h1_ref, xn_ref = refs
        h1_ref[...] = h_ref[...]
    else:
        hl_ref, hc_ref, mod_ref, g_ref, w13_ref, w2_ref, o_ref, h1_ref, xn_ref = refs
        _stream_tile(hl_ref, hc_ref, h1_ref, i, n_lat_tiles)
    d = h1_ref.shape[-1]
    segs = [slice(s * SEG, (s + 1) * SEG) for s in range(geom["segs_per_tile"])]
    rows = [_mod_row(i, s, **geom) for s in range(geom["segs_per_tile"])]
    for sl, row in zip(segs, rows):
        y = jnp.dot(a_ref[sl, :], wo_ref[...], preferred_element_type=F32)
        h1_ref[sl, :] += _mod_vec(mod_ref, row, 2, d) * y
    _modulate_tile(h1_ref, xn_ref, mod_ref, g_ref, i, 3, geom)
    acc = _swiglu(xn_ref[...], w13_ref, w2_ref, tf)
    for sl, row in zip(segs, rows):
        o_ref[sl, :] = h1_ref[sl, :] + _mod_vec(mod_ref, row, 5, d) * acc[sl, :]


def _resident(a):
    return pl.BlockSpec(a.shape, lambda *_: (0,) * a.ndim, pipeline_mode=pl.Buffered(1))


def _mix_ffn(a, wo, h, mod, g, w13, w2, geom, tm, tf):
    t, ka = a.shape
    two = isinstance(h, tuple)
    d = (h[0] if two else h).shape[1]
    n_lat_tiles = h[0].shape[0] // tm if two else None
    h_specs = _stream_specs(tm, d, n_lat_tiles) if two else [pl.BlockSpec((tm, d), lambda i: (i, 0))]
    return pl.pallas_call(
        functools.partial(_mix_ffn_kernel, geom=geom, tf=tf, n_lat_tiles=n_lat_tiles),
        grid=(t // tm,),
        in_specs=[pl.BlockSpec((tm, ka), lambda i: (i, 0)), _resident(wo)] + h_specs
        + [_full(mod), _full(g), _resident(w13), _resident(w2)],
        out_specs=pl.BlockSpec((tm, d), lambda i: (i, 0)),
        out_shape=jax.ShapeDtypeStruct((t, d), F32),
        scratch_shapes=[pltpu.VMEM((tm, d), F32), pltpu.VMEM((tm, d), BF16)],
        compiler_params=_cparams(1),
        name="mix_ffn",
    )(a, wo, *(h if two else (h,)), mod, g, w13, w2)


def _router_kernel(x_ref, mod_ref, g_ref, wr_ref, info_ref, cnt_ref, xn_ref, tri_ref, run_ref, *, geom):
    i = pl.program_id(0)
    tm = x_ref.shape[0]

    @pl.when(i == 0)
    def _():
        r = lax.broadcasted_iota(jnp.int32, (tm, tm), 0)
        c = lax.broadcasted_iota(jnp.int32, (tm, tm), 1)
        tri_ref[...] = jnp.where(c < r, 1.0, 0.0).astype(tri_ref.dtype)
        run_ref[...] = jnp.zeros_like(run_ref)

    _modulate_tile(x_ref, xn_ref, mod_ref, g_ref, i, 3, geom)
    logits = jnp.dot(xn_ref[...], wr_ref[...], preferred_element_type=F32, precision=lax.Precision.HIGHEST)
    n_e = float(logits.shape[1])
    lane = lax.broadcasted_iota(jnp.int32, logits.shape, 1).astype(F32)
    m1 = jnp.max(logits, axis=1, keepdims=True)
    i1 = jnp.min(jnp.where(logits == m1, lane, n_e), axis=1, keepdims=True)
    rest = jnp.where(lane == i1, -jnp.inf, logits)
    m2 = jnp.max(rest, axis=1, keepdims=True)
    i2 = jnp.min(jnp.where(rest == m2, lane, n_e), axis=1, keepdims=True)
    e2 = jnp.exp(m2 - m1)
    w1 = 1.0 / (1.0 + e2)
    w2 = e2 / (1.0 + e2)
    onehot = jnp.where((lane == i1) | (lane == i2), 1.0, 0.0)
    before = jnp.dot(tri_ref[...], onehot.astype(tri_ref.dtype), preferred_element_type=F32) + run_ref[...]
    r1 = jnp.sum(jnp.where(lane == i1, before, 0.0), axis=1, keepdims=True)
    r2 = jnp.sum(jnp.where(lane == i2, before, 0.0), axis=1, keepdims=True)
    run_ref[...] += jnp.sum(onehot, axis=0, keepdims=True)
    cnt_ref[...] = run_ref[...]
    vals = (i1, i2, w1, w2, r1, r2)
    info = jnp.zeros_like(logits)
    for k, val in enumerate(vals):
        info = jnp.where(lane == float(k), val, info)
    info_ref[...] = info


def _router(x, mod, g, wr, geom, tm):
    t, d = x.shape
    n_e = wr.shape[1]
    assert n_e == INFO_COLS
    return pl.pallas_call(
        functools.partial(_router_kernel, geom=geom),
        grid=(t // tm,),
        in_specs=[pl.BlockSpec((tm, d), lambda i: (i, 0)), _full(mod), _full(g), _full(wr)],
        out_specs=[pl.BlockSpec((tm, INFO_COLS), lambda i: (i, 0)),
                   pl.BlockSpec((1, n_e), lambda i: (0, 0))],
        out_shape=[jax.ShapeDtypeStruct((t, INFO_COLS), F32), jax.ShapeDtypeStruct((1, n_e), F32)],
        scratch_shapes=[pltpu.VMEM((tm, d), F32), pltpu.VMEM((tm, tm), BF16), pltpu.VMEM((1, n_e), F32)],
        compiler_params=_cparams(1),
        name="router",
    )(x, mod, g, wr)


EXPERT_TILE_ROWS = 512
DMA_ISSUE_UNROLL = 8


def _row_copy(src_ref, src_row, dst_ref, dst_row, sem):
    return pltpu.make_async_copy(src_ref.at[pl.ds(src_row, 1)], dst_ref.at[pl.ds(dst_row, 1)], sem)


def _dispatch_kernel(slots_ref, ends_ref, x_ref, mod_ref, g_ref, xs_ref, xn_ref, sem, *, geom, n_tiles, rows):
    i = pl.program_id(0)
    tm = x_ref.shape[0]
    n_e = ends_ref.shape[0]

    @pl.when(i == 0)
    def _():
        xn_ref[0:rows, :] = jnp.zeros((rows, xn_ref.shape[1]), xn_ref.dtype)

        def zero_tile(row0):
            cp = pltpu.make_async_copy(xn_ref.at[pl.ds(0, rows)], xs_ref.at[pl.ds(row0, rows)], sem)
            cp.start()
            cp.wait()

        for e in range(n_e):
            start = ends_ref[e - 1] if e else 0
            last_tile = pl.multiple_of(ends_ref[e] - rows, rows)
            pl.when(ends_ref[e] > start)(functools.partial(zero_tile, last_tile))
        for j in range(n_tiles - n_e, n_tiles):
            pl.when(j * rows >= ends_ref[n_e - 1])(functools.partial(zero_tile, j * rows))

    _modulate_tile(x_ref, xn_ref, mod_ref, g_ref, i, 3, geom)

    def body(r, carry):
        for k in range(TOP_K):
            _row_copy(xn_ref, r, xs_ref, slots_ref[TOP_K * r + k], sem).start(priority=k)
        return carry

    lax.fori_loop(0, tm, body, 0, unroll=DMA_ISSUE_UNROLL)
    for k in range(TOP_K):
        pltpu.make_async_copy(xn_ref, xs_ref.at[pl.ds(0, tm)], sem).wait()


def _dispatch(slots, ends, x, mod, g, n_tiles, rows, geom, tm):
    t, d = x.shape
    assert rows <= tm
    return pl.pallas_call(
        functools.partial(_dispatch_kernel, geom=geom, n_tiles=n_tiles, rows=rows),
        grid=(t // tm,),
        in_specs=[pl.BlockSpec((TOP_K * tm,), lambda i: (i,), memory_space=pltpu.SMEM),
                  pl.BlockSpec(memory_space=pltpu.SMEM),
                  pl.BlockSpec((tm, d), lambda i: (i, 0)), _full(mod), _full(g)],
        out_specs=pl.BlockSpec(memory_space=pl.ANY),
        out_shape=jax.ShapeDtypeStruct((n_tiles * rows, d), F32),
        scratch_shapes=[pltpu.VMEM((tm, d), F32), pltpu.SemaphoreType.DMA(())],
        compiler_params=_cparams(1),
        name="moe_dispatch",
    )(slots, ends, x, mod, g)


def _experts_kernel(tile_expert_ref, n_used_ref, xs_ref, w13_ref, w2_ref, y_ref, *, tf):
    del tile_expert_ref
    i = pl.program_id(0)

    @pl.when(i < n_used_ref[0])
    def _():
        y_ref[...] = _swiglu(xs_ref[...].astype(BF16), w13_ref, w2_ref, tf)

    @pl.when(i >= n_used_ref[0])
    def _():
        y_ref[...] = jnp.zeros_like(y_ref)


def _experts(tile_expert, n_used, xs, w13, w2, rows, tf):
    p, d = xs.shape
    _, f_total, _ = w2.shape
    once = pl.Buffered(1)
    grid_spec = pltpu.PrefetchScalarGridSpec(
        num_scalar_prefetch=2,
        grid=(p // rows,),
        in_specs=[pl.BlockSpec((rows, d), lambda i, te, nu: (i, 0)),
                  pl.BlockSpec((None, d, 2 * f_total), lambda i, te, nu: (te[i], 0, 0), pipeline_mode=once),
                  pl.BlockSpec((None, f_total, d), lambda i, te, nu: (te[i], 0, 0), pipeline_mode=once)],
        out_specs=pl.BlockSpec((rows, d), lambda i, te, nu: (i, 0)),
    )
    return pl.pallas_call(
        functools.partial(_experts_kernel, tf=tf),
        grid_spec=grid_spec,
        out_shape=jax.ShapeDtypeStruct((p, d), F32),
        compiler_params=_cparams(1),
        name="moe_experts",
    )(tile_expert, n_used, xs, w13, w2)


def _combine_kernel(slots_ref, x_ref, info_ref, mod_ref, y_ref, *refs, geom):
    fg_ref = refs[0] if len(refs) == 4 else None
    o_ref, yb_ref, sem = refs[-3:]
    i = pl.program_id(0)
    tm, d = x_ref.shape

    def body(r, carry):
        for k in range(TOP_K):
            _row_copy(y_ref, slots_ref[TOP_K * r + k], yb_ref.at[k], r, sem).start(priority=k)
        return carry

    lax.fori_loop(0, tm, body, 0, unroll=DMA_ISSUE_UNROLL)
    for k in range(TOP_K):
        pltpu.make_async_copy(y_ref.at[pl.ds(0, tm)], yb_ref.at[k], sem).wait()

    lane = lax.broadcasted_iota(jnp.int32, (SEG, INFO_COLS), 1)
    for s in range(geom["segs_per_tile"]):
        row = _mod_row(i, s, **geom)
        gate2 = _mod_vec(mod_ref, row, 5, d)
        sl = slice(s * SEG, (s + 1) * SEG)
        info = info_ref[sl, :]
        w1 = jnp.sum(jnp.where(lane == 2, info, 0.0), axis=1, keepdims=True)
        w2 = jnp.sum(jnp.where(lane == 3, info, 0.0), axis=1, keepdims=True)
        out = x_ref[sl, :] + gate2 * (w1 * yb_ref[0, sl, :] + w2 * yb_ref[1, sl, :])
        o_ref[sl, :] = out if fg_ref is None else _rms(out) * fg_ref[...]


def _combine(slots, x, info, mod, y, geom, tm, final=None):
    t, d = x.shape
    n_rows = final[1] if final else t
    extra = [final[0]] if final else []
    return pl.pallas_call(
        functools.partial(_combine_kernel, geom=geom),
        grid=(n_rows // tm,),
        in_specs=[pl.BlockSpec((TOP_K * tm,), lambda i: (i,), memory_space=pltpu.SMEM),
                  pl.BlockSpec((tm, d), lambda i: (i, 0)),
                  pl.BlockSpec((tm, INFO_COLS), lambda i: (i, 0)), _full(mod),
                  pl.BlockSpec(memory_space=pl.ANY)] + [_full(a) for a in extra],
        out_specs=pl.BlockSpec((tm, d), lambda i: (i, 0)),
        out_shape=jax.ShapeDtypeStruct((n_rows, d), F32),
        scratch_shapes=[pltpu.VMEM((TOP_K, tm, d), F32), pltpu.SemaphoreType.DMA(())],
        compiler_params=_cparams(1),
        name="moe_combine",
    )(slots, x, info, mod, y, *extra)


def _moe(x, mod, g, wr, w13, w2, geom, tm, tf, final=None):
    t, d = x.shape
    n_e = wr.shape[1]
    info, counts = _router(x, mod, g, wr, geom, tm)
    rows = EXPERT_TILE_ROWS
    counts = counts[0].astype(jnp.int32)
    padded = (counts + rows - 1) // rows * rows
    ends = jnp.cumsum(padded)
    starts = ends - padded
    n_tiles = TOP_K * t // rows + n_e
    tile_start = jnp.arange(n_tiles, dtype=jnp.int32) * rows
    tile_expert = jnp.minimum(jnp.sum((ends[None, :] <= tile_start[:, None]).astype(jnp.int32), axis=1), n_e - 1)
    n_used = (ends[-1:] // rows).astype(jnp.int32)
    experts = info[:, :TOP_K].astype(jnp.int32)
    ranks = info[:, 4:4 + TOP_K].astype(jnp.int32)
    slots = (starts[experts] + ranks).reshape(-1)
    xs = _dispatch(slots, ends, x, mod, g, n_tiles, rows, geom, tm)
    y = _experts(tile_expert, n_used, xs, w13, w2, rows, tf)
    return _combine(slots, x, info, mod, y, geom, tm, final)


def _proj_b_kernel(x_ref, mod_ref, g_ref, wd_ref, qg_ref, kvg_ref, wq_ref, wk_ref, wv_ref,
                   cos_ref, sa_ref, sb_ref, q_ref, k_ref, v_ref, xn_ref, *, geom, scale):
    i = pl.program_id(0)
    _modulate_tile(x_ref, xn_ref, mod_ref, g_ref, i, 0, geom)
    xn = xn_ref[...]
    cos, sa, sb = cos_ref[...], sa_ref[...], sb_ref[...]
    dq = jnp.dot(xn, wd_ref[:, :B_Q_LORA], preferred_element_type=F32)
    dkv = jnp.dot(xn, wd_ref[:, B_Q_LORA:B_Q_LORA + B_KV_LORA], preferred_element_type=F32)
    kr = jnp.dot(xn, wd_ref[:, B_Q_LORA + B_KV_LORA:], preferred_element_type=F32)
    dqn = (_rms(dq) * qg_ref[...]).astype(BF16)
    dkvn = (_rms(dkv) * kvg_ref[...]).astype(BF16)
    kr = _rope(kr, cos, sa, sb, B_ROPE_DIM // 4).astype(k_ref.dtype)
    for h in range(B_HEADS):
        qh = jnp.dot(dqn, wq_ref[:, h * MXU_N:(h + 1) * MXU_N], preferred_element_type=F32)
        q_ref[:, h * MXU_N:h * MXU_N + LANES] = (qh[:, :LANES] * scale).astype(q_ref.dtype)
        q_ref[:, h * MXU_N + LANES:(h + 1) * MXU_N] = (
            _rope(qh[:, LANES:], cos, sa, sb, B_ROPE_DIM // 4) * scale).astype(q_ref.dtype)
        k_ref[:, h * MXU_N + LANES:(h + 1) * MXU_N] = kr
    for pair in range(B_HEADS // 2):
        sl = slice(pair * MXU_N, (pair + 1) * MXU_N)
        kn = jnp.dot(dkvn, wk_ref[:, sl], preferred_element_type=F32)
        for half in range(2):
            h = 2 * pair + half
            k_ref[:, h * MXU_N:h * MXU_N + LANES] = kn[:, half * LANES:(half + 1) * LANES].astype(k_ref.dtype)
        v_ref[:, sl] = jnp.dot(dkvn, wv_ref[:, sl], preferred_element_type=F32).astype(v_ref.dtype)


def _proj_b(x, mod, g, wd, qg, kvg, wq, wk, wv, tabs, geom, tm):
    t, d = x.shape
    rows = lambda width: pl.BlockSpec((tm, width), lambda i: (i, 0))
    nq, nv = B_HEADS * MXU_N, B_HEADS * B_V_DIM
    return pl.pallas_call(
        functools.partial(_proj_b_kernel, geom=geom, scale=(B_NOPE_DIM + B_ROPE_DIM) ** -0.5 * LOG2E),
        grid=(t // tm,),
        in_specs=[rows(d), _full(mod), _full(g), _full(wd), _full(qg), _full(kvg), _full(wq), _full(wk),
                  _full(wv), rows(LANES), rows(LANES), rows(LANES)],
        out_specs=[rows(nq), rows(nq), rows(nv)],
        out_shape=[jax.ShapeDtypeStruct((t, nq), BF16), jax.ShapeDtypeStruct((t, nq), BF16),
                   jax.ShapeDtypeStruct((t, nv), BF16)],
        scratch_shapes=[pltpu.VMEM((tm, d), BF16)],
        compiler_params=_cparams(1),
        name="proj_b",
    )(x, mod, g, wd, qg, kvg, wq, wk, wv, *tabs)


def _gelu(x):
    return 0.5 * x * (1.0 + lax.erf(x * (2.0 ** -0.5)))


def _sgu_kernel(x_ref, mod_ref, g_ref, win_ref, lng_ref, lnb_ref, ws_ref, bs_ref, gated_ref,
                xn_ref, u_ref, v_ref, *, geom):
    i = pl.program_id(0)
    tm = x_ref.shape[0]
    cw = u_ref.shape[1]
    _modulate_tile(x_ref, xn_ref, mod_ref, g_ref, i, 0, geom)
    xn = xn_ref[...]
    for j in range(cw // MXU_N):
        sl = slice(j * MXU_N, (j + 1) * MXU_N)
        u_ref[:, sl] = _gelu(jnp.dot(xn, win_ref[:, sl], preferred_element_type=F32))
        v_ref[:, sl] = _gelu(jnp.dot(xn, win_ref[:, cw + j * MXU_N:cw + (j + 1) * MXU_N],
                                     preferred_element_type=F32))
    lng, lnb = lng_ref[...], lnb_ref[...]
    gw = cw // C_GROUPS
    for c in range(tm // C_CHUNK):
        rs = slice(c * C_CHUNK, (c + 1) * C_CHUNK)
        v = v_ref[rs, :]
        mu = _row_sum(v) * (1.0 / cw)
        vc = v - mu
        var = _row_sum(vc * vc) * (1.0 / cw)
        vn = (vc * lax.rsqrt(var + EPS) * lng + lnb).astype(BF16)
        for gi in range(C_GROUPS):
            cs = slice(gi * gw, (gi + 1) * gw)
            mixed = jnp.dot(ws_ref[gi], vn[:, cs], preferred_element_type=F32) + bs_ref[:, cs]
            gated_ref[rs, cs] = (u_ref[rs, cs] * mixed).astype(gated_ref.dtype)


def _sgu(x, mod, g, win, lng, lnb, ws, bs, geom, tm):
    t, d = x.shape
    cw = win.shape[1] // 2
    return pl.pallas_call(
        functools.partial(_sgu_kernel, geom=geom),
        grid=(t // tm,),
        in_specs=[pl.BlockSpec((tm, d), lambda i: (i, 0)), _full(mod), _full(g), _full(win), _full(lng),
                  _full(lnb), _full(ws), _full(bs)],
        out_specs=pl.BlockSpec((tm, cw), lambda i: (i, 0)),
        out_shape=jax.ShapeDtypeStruct((t, cw), BF16),
        scratch_shapes=[pltpu.VMEM((tm, d), BF16), pltpu.VMEM((tm, cw), F32), pltpu.VMEM((tm, cw), F32)],
        compiler_params=_cparams(1),
        name="sgu",
    )(x, mod, g, win, lng, lnb, ws, bs)


def _proj_d_kernel(x_ref, mod_ref, g_ref, w_ref, cos_ref, sa_ref, sb_ref, q_ref, k_ref, v_ref, xn_ref,
                   *, geom, scale):
    i = pl.program_id(0)
    _modulate_tile(x_ref, xn_ref, mod_ref, g_ref, i, 0, geom)
    xn = xn_ref[...]
    cos, sa, sb = cos_ref[...], sa_ref[...], sb_ref[...]
    nq = q_ref.shape[1]
    for pair in range(nq // MXU_N):
        y2 = jnp.dot(xn, w_ref[:, pair * MXU_N:(pair + 1) * MXU_N], preferred_element_type=F32)
        for half in range(2):
            sl = slice(half * LANES, (half + 1) * LANES)
            y = _rope(y2[:, sl], cos, sa, sb, D_HEAD_DIM // 4) * scale
            q_ref[:, pair * MXU_N + half * LANES:pair * MXU_N + (half + 1) * LANES] = y.astype(q_ref.dtype)
    kv = jnp.dot(xn, w_ref[:, nq:nq + MXU_N], preferred_element_type=F32)
    k_ref[...] = _rope(kv[:, :LANES], cos, sa, sb, D_HEAD_DIM // 4).astype(k_ref.dtype)
    v_ref[...] = kv[:, LANES:].astype(v_ref.dtype)


def _proj_d(x, mod, g, w, tabs, geom, tm):
    t, d = x.shape
    nq = D_HEADS * D_HEAD_DIM
    rows = lambda width: pl.BlockSpec((tm, width), lambda i: (i, 0))
    return pl.pallas_call(
        functools.partial(_proj_d_kernel, geom=geom, scale=D_HEAD_DIM ** -0.5 * LOG2E),
        grid=(t // tm,),
        in_specs=[rows(d), _full(mod), _full(g), _full(w), rows(LANES), rows(LANES), rows(LANES)],
        out_specs=[rows(nq), rows(LANES), rows(LANES)],
        out_shape=[jax.ShapeDtypeStruct((t, nq), BF16), jax.ShapeDtypeStruct((t, LANES), BF16),
                   jax.ShapeDtypeStruct((t, LANES), BF16)],
        scratch_shapes=[pltpu.VMEM((tm, d), BF16)],
        compiler_params=_cparams(1),
        name="proj_d",
    )(x, mod, g, w, *tabs)


WIN_STACK = 4


def _win_kernel(sink_ref, q_ref, kc_ref, km_ref, k0_ref, kp_ref, vc_ref, vm_ref, v0_ref, vp_ref, o_ref,
                *, lat_blocks, blocks_per_sample):
    j = pl.program_id(0)
    blk = q_ref.shape[0]
    ctx = kc_ref.shape[0]
    kk = jnp.concatenate([kc_ref[...], km_ref[...], k0_ref[...], kp_ref[...]], axis=0)
    vv = jnp.concatenate([vc_ref[...], vm_ref[...], v0_ref[...], vp_ref[...]], axis=0)
    n_keys = kk.shape[0]
    n = lax.rem(j, jnp.int32(blocks_per_sample))
    qpos = lax.broadcasted_iota(jnp.int32, (blk, n_keys), 0)
    col = lax.broadcasted_iota(jnp.int32, (blk, n_keys), 1)
    rel = col - ctx - blk - qpos
    kblock = n - 1 + lax.shift_right_arithmetic(col - ctx, int(math.log2(blk)))
    far = jnp.where(j < lat_blocks, 0, 2 * D_WINDOW + 2)
    in_window = (jnp.abs(rel) + far <= D_WINDOW) & (kblock >= 0) & (kblock < blocks_per_sample)
    mask = (col < ctx) | in_window
    group = D_HEADS // D_KV_HEADS
    bias = jnp.concatenate([jnp.where(mask, 0.0, MASK_VALUE)] * WIN_STACK, axis=0)
    for h0 in range(0, D_HEADS, WIN_STACK):
        hk = h0 // group
        heads = range(h0, h0 + WIN_STACK)
        q = jnp.concatenate([q_ref[:, h * D_HEAD_DIM:(h + 1) * D_HEAD_DIM] for h in heads], axis=0)
        sink = jnp.concatenate([jnp.full((blk, 1), sink_ref[h] * LOG2E, F32) for h in heads], axis=0)
        k = kk[:, hk * D_HEAD_DIM:(hk + 1) * D_HEAD_DIM]
        v = vv[:, hk * D_HEAD_DIM:(hk + 1) * D_HEAD_DIM]
        v_ones = jnp.concatenate([v, jnp.ones_like(v)], axis=1)
        s = lax.dot_general(q, k, (((1,), (1,)), ((), ())), preferred_element_type=F32) + bias
        m = jnp.maximum(_row_max(s), sink)
        e = jnp.exp2((s - m).astype(v.dtype))
        pv = jnp.dot(e, v_ones, preferred_element_type=F32)
        o = pv[:, :D_HEAD_DIM] / (pv[:, D_HEAD_DIM:] + jnp.exp2(sink - m))
        for n_h, h in enumerate(heads):
            o_ref[:, h * D_HEAD_DIM:(h + 1) * D_HEAD_DIM] = o[n_h * blk:(n_h + 1) * blk, :].astype(o_ref.dtype)


def _win_attn(sinks, q, k, v, *, n_batch, lat, ctx, blk=128):
    t, nq = q.shape
    bps = lat // blk
    lat_blocks = n_batch * bps
    ctx_bps = ctx // blk
    ctx_blk0 = n_batch * lat // ctx

    def sample(j):
        return jnp.where(j < lat_blocks, j // bps, (j - lat_blocks) // ctx_bps)

    def neighbour(j, off):
        b = sample(j)
        n = jnp.where(j < lat_blocks, j - b * bps, 0)
        return b * bps + jnp.clip(n + off, 0, bps - 1)

    kv_specs = [pl.BlockSpec((ctx, LANES), lambda j: (ctx_blk0 + sample(j), 0)),
                pl.BlockSpec((blk, LANES), lambda j: (neighbour(j, -1), 0)),
                pl.BlockSpec((blk, LANES), lambda j: (neighbour(j, 0), 0)),
                pl.BlockSpec((blk, LANES), lambda j: (neighbour(j, 1), 0))]
    return pl.pallas_call(
        functools.partial(_win_kernel, lat_blocks=lat_blocks, blocks_per_sample=bps),
        grid=(t // blk,),
        in_specs=[pl.BlockSpec(memory_space=pltpu.SMEM),
                  pl.BlockSpec((blk, nq), lambda j: (j, 0))] + kv_specs + kv_specs,
        out_specs=pl.BlockSpec((blk, nq), lambda j: (j, 0)),
        out_shape=jax.ShapeDtypeStruct((t, nq), BF16),
        compiler_params=_cparams(1),
        name="win_attn",
    )(sinks, q, k, k, k, k, v, v, v, v)


def _rope_tables(n_batch, lat, ctx, dim, tiled):
    quarter = dim // 4
    pos = jnp.arange(lat, dtype=jnp.int32)
    rows, cols = pos // GRID_W, pos % GRID_W
    inv_freq = ROPE_THETA ** (-jnp.arange(quarter, dtype=F32) / quarter)
    ang_r = rows.astype(F32)[:, None] * inv_freq
    ang_c = cols.astype(F32)[:, None] * inv_freq
    ang = jnp.concatenate([ang_r, ang_r, ang_c, ang_c], axis=-1)
    cos, sin = jnp.cos(ang), jnp.sin(ang)
    first = (jnp.arange(dim) % (2 * quarter)) < quarter
    out = []
    for tab, fill in ((cos, 1.0), (jnp.where(first, -sin, 0.0), 0.0), (jnp.where(first, 0.0, sin), 0.0)):
        if tiled:
            tab = jnp.tile(tab, (1, LANES // dim))
        else:
            tab = jnp.concatenate([tab, jnp.full((lat, LANES - dim), fill, F32)], axis=1)
        out.append(jnp.concatenate([jnp.tile(tab, (n_batch, 1)), jnp.full((n_batch * ctx, LANES), fill, F32)],
                                   axis=0))
    return tuple(out)


def _pick_tf(f, cap):
    return max(tf for tf in range(LANES, cap + 1, LANES) if f % tf == 0)


def kernel(x, c, ctx, c_ctx, ada_w, ada_b, norm_mix, norm_ffn, final_norm, a_wqkv, a_q_norm, a_k_norm, a_wo, b_w_down, b_q_lora_norm, b_kv_lora_norm, b_w_uq, b_w_ukv, b_wo, c_w_in, c_ln_g, c_ln_b, c_w_spatial, c_b_spatial, c_w_out, d_wqkv, d_sinks, d_wo, ffn_w13, ffn_w2, moe_router, moe_w13, moe_w2):
    n_batch, lat, d = x.shape
    n_ctx = ctx.shape[1]
    depth = ada_w.shape[0]
    assert n_ctx == SEG and lat % 1024 == 0 and n_batch < MOD_ROWS and depth == 4
    ctx_segs = n_batch * n_ctx // SEG
    segs_per_tile = max(k for k in (4, 2, 1) if ctx_segs % k == 0)
    tm = segs_per_tile * SEG
    geom = dict(segs_per_tile=segs_per_tile, lat_segs=n_batch * lat // SEG, segs_per_sample=lat // SEG,
                n_batch=n_batch)
    half_segs = min(2, segs_per_tile)
    geom_half, tm_half = dict(geom, segs_per_tile=half_segs), half_segs * SEG

    cvec = jnp.concatenate([c, c_ctx[None, :], jnp.zeros((MOD_ROWS - n_batch - 1, d), F32)], axis=0)
    mod = _ada_mod(cvec, ada_w, ada_b)
    h = (x.reshape(n_batch * lat, d), ctx.reshape(n_batch * n_ctx, d))

    tabs_a = _rope_tables(n_batch, lat, n_ctx, A_HEAD_DIM, True)
    tabs_d = _rope_tables(n_batch, lat, n_ctx, D_HEAD_DIM, True)
    tabs_b = _rope_tables(n_batch, lat, n_ctx, B_ROPE_DIM, False)
    row = lambda v: v.reshape(1, -1)
    dense_tf = _pick_tf(ffn_w13.shape[-1] // 2, 256)
    moe_tf = _pick_tf(moe_w13.shape[-1] // 2, 512)
    attn = functools.partial(_flash, n_batch=n_batch, lat=lat, ctx=n_ctx)

    qkv = _proj_a(*h, mod[0], row(norm_mix[0]), a_wqkv[0].astype(BF16), row(a_q_norm[0]), row(a_k_norm[0]),
                  tabs_a, geom, tm)
    grp = A_HEADS // A_KV_HEADS
    o = attn(qkv, qkv, qkv, n_groups=A_KV_HEADS, heads=grp, dq=A_HEAD_DIM, dv=A_HEAD_DIM,
             k_col=lambda g: A_HEADS + g, v_col=lambda g: A_HEADS + A_KV_HEADS + g, tq=256)
    h = _mix_ffn(o, a_wo[0].astype(BF16), h, mod[0], row(norm_ffn[0]), ffn_w13[0].astype(BF16),
                 ffn_w2[0].astype(BF16), geom_half, tm_half, dense_tf)

    wd = b_w_down[0]
    wd = jnp.concatenate([wd, jnp.zeros((d, LANES - B_ROPE_DIM), F32)], axis=1).astype(BF16)
    wq = b_w_uq[0].reshape(B_Q_LORA, B_HEADS, B_NOPE_DIM + B_ROPE_DIM)
    wq = jnp.concatenate([wq, jnp.zeros((B_Q_LORA, B_HEADS, MXU_N - B_NOPE_DIM - B_ROPE_DIM), F32)], axis=2)
    wq = wq.reshape(B_Q_LORA, B_HEADS * MXU_N).astype(BF16)
    wkv = b_w_ukv[0].reshape(B_KV_LORA, B_HEADS, B_NOPE_DIM + B_V_DIM)
    wk = wkv[:, :, :B_NOPE_DIM].reshape(B_KV_LORA, B_HEADS * B_NOPE_DIM).astype(BF16)
    wv = wkv[:, :, B_NOPE_DIM:].reshape(B_KV_LORA, B_HEADS * B_V_DIM).astype(BF16)
    qc, kc, vb = _proj_b(h, mod[1], row(norm_mix[1]), wd, row(b_q_lora_norm[0]), row(b_kv_lora_norm[0]),
                         wq, wk, wv, tabs_b, geom, tm)
    o = attn(qc, kc, vb, n_groups=B_HEADS, heads=1, dq=MXU_N, dv=B_V_DIM,
             k_col=lambda g: g, v_col=lambda g: g, tq=1024)
    h = _proj_res(o, b_wo[0].astype(BF16), h, mod[1], geom, tm)
    h = _moe(h, mod[1], row(norm_ffn[1]), moe_router[0], moe_w13[0].astype(BF16), moe_w2[0].astype(BF16),
             geom, tm, moe_tf)

    bs = jnp.repeat(c_b_spatial[0].T, c_w_out.shape[1] // C_GROUPS, axis=1)
    gated = _sgu(h, mod[2], row(norm_mix[2]), c_w_in[0].astype(BF16), row(c_ln_g[0]), row(c_ln_b[0]),
                 c_w_spatial[0].astype(BF16), bs, geom_half, tm_half)
    h = _mix_ffn(gated, c_w_out[0].astype(BF16), h, mod[2], row(norm_ffn[2]), ffn_w13[1].astype(BF16),
                 ffn_w2[1].astype(BF16), geom_half, tm_half, dense_tf)

    q3, k3, v3 = _proj_d(h, mod[3], row(norm_mix[3]), d_wqkv[0].astype(BF16), tabs_d, geom, tm)
    o = _win_attn(d_sinks[0], q3, k3, v3, n_batch=n_batch, lat=lat, ctx=n_ctx)
    h = _proj_res(o, d_wo[0].astype(BF16), h, mod[3], geom, tm)
    out = _moe(h, mod[3], row(norm_ffn[3]), moe_router[1], moe_w13[1].astype(BF16), moe_w2[1].astype(BF16),
               geom, tm, moe_tf, final=(row(final_norm), n_batch * lat))
    return out.reshape(n_batch, lat, d)
```

```python
import functools
import math

import jax
import jax.numpy as jnp
from jax import lax
from jax.experimental import pallas as pl
from jax.experimental.pallas import tpu as pltpu

F32 = jnp.float32
BF16 = jnp.bfloat16

EPS = 1e-6
ROPE_THETA = 10000.0
GRID_W = 64
MASK_VALUE = -1e30
LOG2E = math.log2(math.e)

A_HEADS, A_KV_HEADS, A_HEAD_DIM = 8, 2, 128
B_HEADS, B_Q_LORA, B_KV_LORA, B_NOPE_DIM, B_ROPE_DIM, B_V_DIM = 8, 384, 256, 128, 64, 128
C_GROUPS, C_CHUNK = 8, 128
D_HEADS, D_KV_HEADS, D_HEAD_DIM, D_WINDOW = 16, 2, 64, 128
TOP_K = 2

SEG = 256
LANES = 128
MXU_N = 256
VMEM_LIMIT = 56 * 1024 * 1024

MOD_ROWS = 8
INFO_COLS = 8


def _cparams(n_axes):
    return pltpu.CompilerParams(dimension_semantics=("arbitrary",) * n_axes,
                                vmem_limit_bytes=VMEM_LIMIT)


def _full(a):
    return pl.BlockSpec(a.shape, lambda *_: (0,) * a.ndim)


def _mod_row(tile, s, segs_per_tile, lat_segs, segs_per_sample, n_batch):
    g = tile * segs_per_tile + s
    return jnp.where(g < lat_segs, lax.div(g, jnp.int32(segs_per_sample)), n_batch)


def _mod_vec(mod_ref, row, k, d):
    return mod_ref[pl.ds(row, 1), k * d:(k + 1) * d]


def _row_sum(x):
    return jnp.sum(x, axis=-1, keepdims=True)


def _row_max(x):
    return jnp.max(x, axis=-1, keepdims=True)


def _rms(x):
    return x * lax.rsqrt(jnp.mean(x * x, axis=-1, keepdims=True) + EPS)


def _modulate_tile(x_ref, xn_ref, mod_ref, g_ref, tile, k_shift, geom):
    d = x_ref.shape[-1]
    g = g_ref[...]
    for s in range(geom["segs_per_tile"]):
        row = _mod_row(tile, s, **geom)
        shift = _mod_vec(mod_ref, row, k_shift, d)
        scale = _mod_vec(mod_ref, row, k_shift + 1, d)
        x = x_ref[s * SEG:(s + 1) * SEG, :]
        y = _rms(x) * g * (1.0 + scale) + shift
        xn_ref[s * SEG:(s + 1) * SEG, :] = y.astype(xn_ref.dtype)


def _rope(t, cos, sin_a, sin_b, shift):
    return (t * cos + pltpu.roll(t, LANES - shift, axis=1) * sin_a
            + pltpu.roll(t, shift, axis=1) * sin_b)


def _ada_kernel(c_ref, w_ref, b_ref, o_ref):
    c = c_ref[...]
    a = c * jax.nn.sigmoid(c)
    o_ref[0] = jnp.dot(a, w_ref[0], preferred_element_type=F32,
                       precision=lax.Precision.HIGHEST) + b_ref[0]


def _ada_mod(cvec, ada_w, ada_b):
    depth, d, m = ada_w.shape
    tn = m // 4
    return pl.pallas_call(
        _ada_kernel,
        grid=(depth, m // tn),
        in_specs=[pl.BlockSpec((MOD_ROWS, d), lambda l, j: (0, 0)),
                  pl.BlockSpec((1, d, tn), lambda l, j: (l, 0, j)),
                  pl.BlockSpec((1, 1, tn), lambda l, j: (l, 0, j))],
        out_specs=pl.BlockSpec((1, MOD_ROWS, tn), lambda l, j: (l, 0, j)),
        out_shape=jax.ShapeDtypeStruct((depth, MOD_ROWS, m), F32),
        compiler_params=_cparams(2),
        name="ada_mod",
    )(cvec, ada_w, ada_b.reshape(depth, 1, m))


def _stream_tile(xl_ref, xc_ref, buf_ref, i, n_lat_tiles):
    @pl.when(i < n_lat_tiles)
    def _():
        buf_ref[...] = xl_ref[...]

    @pl.when(i >= n_lat_tiles)
    def _():
        buf_ref[...] = xc_ref[...]


def _stream_specs(tm, d, n_lat_tiles):
    return [pl.BlockSpec((tm, d), lambda i: (jnp.minimum(i, n_lat_tiles - 1), 0)),
            pl.BlockSpec((tm, d), lambda i: (jnp.maximum(i - n_lat_tiles, 0), 0))]


def _proj_a_kernel(xl_ref, xc_ref, mod_ref, g_ref, w_ref, qn_ref, kn_ref, cos_ref, sa_ref, sb_ref, o_ref,
                   x_ref, xn_ref, *, geom, scale, n_lat_tiles):
    i = pl.program_id(0)
    _stream_tile(xl_ref, xc_ref, x_ref, i, n_lat_tiles)
    _modulate_tile(x_ref, xn_ref, mod_ref, g_ref, i, 0, geom)
    xn = xn_ref[...]
    cos, sa, sb = cos_ref[...], sa_ref[...], sb_ref[...]
    qn, kn = qn_ref[...], kn_ref[...]
    n_q, n_k = A_HEADS, A_KV_HEADS
    n_heads = n_q + 2 * n_k
    for pair in range(n_heads // 2):
        y2 = jnp.dot(xn, w_ref[:, pair * MXU_N:(pair + 1) * MXU_N], preferred_element_type=F32)
        for half in range(2):
            h = 2 * pair + half
            y = y2[:, half * LANES:(half + 1) * LANES]
            if h < n_q:
                y = _rope(_rms(y) * qn, cos, sa, sb, A_HEAD_DIM // 4) * scale
            elif h < n_q + n_k:
                y = _rope(_rms(y) * kn, cos, sa, sb, A_HEAD_DIM // 4)
            o_ref[:, h * LANES:(h + 1) * LANES] = y.astype(o_ref.dtype)


def _proj_a(xl, xc, mod, g, w, qn, kn, tabs, geom, tm):
    d = xl.shape[1]
    t = xl.shape[0] + xc.shape[0]
    n = w.shape[1]
    n_lat_tiles = xl.shape[0] // tm
    rows = lambda width: pl.BlockSpec((tm, width), lambda i: (i, 0))
    return pl.pallas_call(
        functools.partial(_proj_a_kernel, geom=geom, scale=A_HEAD_DIM ** -0.5 * LOG2E, n_lat_tiles=n_lat_tiles),
        grid=(t // tm,),
        in_specs=_stream_specs(tm, d, n_lat_tiles) + [_full(mod), _full(g), _full(w), _full(qn), _full(kn),
                                                       rows(LANES), rows(LANES), rows(LANES)],
        out_specs=rows(n),
        out_shape=jax.ShapeDtypeStruct((t, n), BF16),
        scratch_shapes=[pltpu.VMEM((tm, d), F32), pltpu.VMEM((tm, d), BF16)],
        compiler_params=_cparams(1),
        name="proj_a",
    )(xl, xc, mod, g, w, qn, kn, *tabs)


FLASH_ROWS = 2048
FLASH_UNROLL = 8


def _flash_kernel(*refs, heads, dq, dv, tk, n_lat_chunks):
    if n_lat_chunks:
        _, q_ref, kc_ref, vc_ref, kl_ref, vl_ref, o_ref = refs
    else:
        _, q_ref, kc_ref, vc_ref, o_ref = refs
    tq = q_ref.shape[0]
    if heads > 1:
        q = jnp.concatenate([q_ref[:, g * dq:(g + 1) * dq] for g in range(heads)], axis=0)
    else:
        q = q_ref[...]
    rows = heads * tq

    def scores(k):
        return lax.dot_general(q, k, (((1,), (1,)), ((), ())), preferred_element_type=F32)

    def update(s, v, m, acc):
        m_new = jnp.maximum(m, _row_max(s))
        alpha = jnp.exp2(m - m_new)
        p = jnp.exp2((s - m_new).astype(v.dtype))
        v_ones = jnp.concatenate([v, jnp.ones_like(v)], axis=1)
        acc = alpha * acc + jnp.dot(p, v_ones, preferred_element_type=F32)
        return m_new, acc

    m0 = jnp.full((rows, 1), MASK_VALUE, F32)
    acc0 = jnp.zeros((rows, 2 * dv), F32)
    m, acc = update(scores(kc_ref[...]), vc_ref[...], m0, acc0)
    if n_lat_chunks:
        def body(c, carry):
            off = pl.multiple_of(c * tk, tk)
            return update(scores(kl_ref[pl.ds(off, tk), :]), vl_ref[pl.ds(off, tk), :], *carry)

        m, acc = lax.fori_loop(0, n_lat_chunks, body, (m, acc), unroll=FLASH_UNROLL)
    o = acc[:, :dv] / acc[:, dv:]
    for g in range(heads):
        o_ref[:, g * dv:(g + 1) * dv] = o[g * tq:(g + 1) * tq, :].astype(o_ref.dtype)


def _flash(q, k, v, *, n_batch, lat, ctx, n_groups, heads, dq, dv, k_col, v_col, tq, tk=512):
    t = q.shape[0]
    nq = lat // tq
    ctx_blk0 = n_batch * lat // ctx
    kern = functools.partial(_flash_kernel, heads=heads, dq=dq, dv=dv, tk=tk)
    out_shape = jax.ShapeDtypeStruct((t, n_groups * heads * dv), BF16)
    o = pl.pallas_call(
        functools.partial(kern, n_lat_chunks=lat // tk),
        grid=(n_batch, n_groups, nq),
        in_specs=[pl.BlockSpec(memory_space=pl.ANY),
                  pl.BlockSpec((tq, heads * dq), lambda b, g, i: (b * nq + i, g)),
                  pl.BlockSpec((ctx, dq), lambda b, g, i: (ctx_blk0 + b, k_col(g))),
                  pl.BlockSpec((ctx, dv), lambda b, g, i: (ctx_blk0 + b, v_col(g))),
                  pl.BlockSpec((lat, dq), lambda b, g, i: (b, k_col(g))),
                  pl.BlockSpec((lat, dv), lambda b, g, i: (b, v_col(g)))],
        out_specs=pl.BlockSpec((tq, heads * dv), lambda b, g, i: (b * nq + i, g)),
        out_shape=out_shape,
        input_output_aliases={0: 0},
        compiler_params=_cparams(3),
        name="flash_lat",
    )(jnp.zeros(out_shape.shape, out_shape.dtype), q, k, v, k, v)
    return pl.pallas_call(
        functools.partial(kern, n_lat_chunks=0),
        grid=(n_batch, n_groups),
        in_specs=[pl.BlockSpec(memory_space=pl.ANY),
                  pl.BlockSpec((ctx, heads * dq), lambda b, g: (ctx_blk0 + b, g)),
                  pl.BlockSpec((ctx, dq), lambda b, g: (ctx_blk0 + b, k_col(g))),
                  pl.BlockSpec((ctx, dv), lambda b, g: (ctx_blk0 + b, v_col(g)))],
        out_specs=pl.BlockSpec((ctx, heads * dv), lambda b, g: (ctx_blk0 + b, g)),
        out_shape=out_shape,
        input_output_aliases={0: 0},
        compiler_params=_cparams(2),
        name="flash_ctx",
    )(o, q, k, v)


def _swiglu(xn, w13_ref, w2_ref, tf):
    f_total = w2_ref.shape[0]
    acc = None
    for f in range(f_total // tf):
        a = jnp.dot(xn, w13_ref[:, f * tf:(f + 1) * tf], preferred_element_type=F32)
        b = jnp.dot(xn, w13_ref[:, f_total + f * tf:f_total + (f + 1) * tf], preferred_element_type=F32)
        y = (a * jax.nn.sigmoid(a) * b).astype(BF16)
        part = jnp.dot(y, w2_ref[f * tf:(f + 1) * tf, :], preferred_element_type=F32)
        acc = part if acc is None else acc + part
    return acc


def _mix_ffn_kernel(a_ref, wo_ref, *refs, geom, tf, n_lat_tiles):
    i = pl.program_id(0)
    if n_lat_tiles is None:
        h_ref, mod_ref, g_ref, w13_ref, w2_ref, o_ref, h1_ref, xn_ref = refs
        h1_ref[...] = h_ref[...]
    else:
        hl_ref, hc_ref, mod_ref, g_ref, w13_ref, w2_ref, o_ref, h1_ref, xn_ref = refs
        _stream_tile(hl_ref, hc_ref, h1_ref, i, n_lat_tiles)
    d = h1_ref.shape[-1]
    segs = [slice(s * SEG, (s + 1) * SEG) for s in range(geom["segs_per_tile"])]
    rows = [_mod_row(i, s, **geom) for s in range(geom["segs_per_tile"])]
    for sl, row in zip(segs, rows):
        y = jnp.dot(a_ref[sl, :], wo_ref[...], preferred_element_type=F32)
        h1_ref[sl, :] += _mod_vec(mod_ref, row, 2, d) * y
    _modulate_tile(h1_ref, xn_ref, mod_ref, g_ref, i, 3, geom)
    acc = _swiglu(xn_ref[...], w13_ref, w2_ref, tf)
    for sl, row in zip(segs, rows):
        o_ref[sl, :] = h1_ref[sl, :] + _mod_vec(mod_ref, row, 5, d) * acc[sl, :]


def _resident(a):
    return pl.BlockSpec(a.shape, lambda *_: (0,) * a.ndim, pipeline_mode=pl.Buffered(1))


def _mix_ffn(a, wo, h, mod, g, w13, w2, geom, tm, tf):
    t, ka = a.shape
    two = isinstance(h, tuple)
    d = (h[0] if two else h).shape[1]
    n_lat_tiles = h[0].shape[0] // tm if two else None
    h_specs = _stream_specs(tm, d, n_lat_tiles) if two else [pl.BlockSpec((tm, d), lambda i: (i, 0))]
    return pl.pallas_call(
        functools.partial(_mix_ffn_kernel, geom=geom, tf=tf, n_lat_tiles=n_lat_tiles),
        grid=(t // tm,),
        in_specs=[pl.BlockSpec((tm, ka), lambda i: (i, 0)), _resident(wo)] + h_specs
        + [_full(mod), _full(g), _resident(w13), _resident(w2)],
        out_specs=pl.BlockSpec((tm, d), lambda i: (i, 0)),
        out_shape=jax.ShapeDtypeStruct((t, d), F32),
        scratch_shapes=[pltpu.VMEM((tm, d), F32), pltpu.VMEM((tm, d), BF16)],
        compiler_params=_cparams(1),
        name="mix_ffn",
    )(a, wo, *(h if two else (h,)), mod, g, w13, w2)


def _router_kernel(a_ref, wo_ref, h_ref, mod_ref, g_ref, wr_ref, x_ref, info_ref, cnt_ref, xn_ref, tri_ref,
                   run_ref, *, geom):
    i = pl.program_id(0)
    tm, d = x_ref.shape

    @pl.when(i == 0)
    def _():
        r = lax.broadcasted_iota(jnp.int32, (tm, tm), 0)
        c = lax.broadcasted_iota(jnp.int32, (tm, tm), 1)
        tri_ref[...] = jnp.where(c < r, 1.0, 0.0).astype(tri_ref.dtype)
        run_ref[...] = jnp.zeros_like(run_ref)

    for s in range(geom["segs_per_tile"]):
        gate = _mod_vec(mod_ref, _mod_row(i, s, **geom), 2, d)
        sl = slice(s * SEG, (s + 1) * SEG)
        x_ref[sl, :] = h_ref[sl, :] + gate * jnp.dot(a_ref[sl, :], wo_ref[...], preferred_element_type=F32)
    _modulate_tile(x_ref, xn_ref, mod_ref, g_ref, i, 3, geom)
    logits = jnp.dot(xn_ref[...], wr_ref[...], preferred_element_type=F32, precision=lax.Precision.HIGHEST)
    n_e = float(logits.shape[1])
    lane = lax.broadcasted_iota(jnp.int32, logits.shape, 1).astype(F32)
    m1 = jnp.max(logits, axis=1, keepdims=True)
    i1 = jnp.min(jnp.where(logits == m1, lane, n_e), axis=1, keepdims=True)
    rest = jnp.where(lane == i1, -jnp.inf, logits)
    m2 = jnp.max(rest, axis=1, keepdims=True)
    i2 = jnp.min(jnp.where(rest == m2, lane, n_e), axis=1, keepdims=True)
    e2 = jnp.exp(m2 - m1)
    w1 = 1.0 / (1.0 + e2)
    w2 = e2 / (1.0 + e2)
    onehot = jnp.where((lane == i1) | (lane == i2), 1.0, 0.0)
    before = jnp.dot(tri_ref[...], onehot.astype(tri_ref.dtype), preferred_element_type=F32) + run_ref[...]
    r1 = jnp.sum(jnp.where(lane == i1, before, 0.0), axis=1, keepdims=True)
    r2 = jnp.sum(jnp.where(lane == i2, before, 0.0), axis=1, keepdims=True)
    run_ref[...] += jnp.sum(onehot, axis=0, keepdims=True)
    cnt_ref[...] = run_ref[...]
    vals = (i1, i2, w1, w2, r1, r2)
    info = jnp.zeros_like(logits)
    for k, val in enumerate(vals):
        info = jnp.where(lane == float(k), val, info)
    info_ref[...] = info


def _router(a, wo, h, mod, g, wr, geom, tm):
    t, d = h.shape
    ka = a.shape[1]
    n_e = wr.shape[1]
    assert n_e == INFO_COLS
    rows = lambda width: pl.BlockSpec((tm, width), lambda i: (i, 0))
    return pl.pallas_call(
        functools.partial(_router_kernel, geom=geom),
        grid=(t // tm,),
        in_specs=[rows(ka), _full(wo), rows(d), _full(mod), _full(g), _full(wr)],
        out_specs=[rows(d), rows(INFO_COLS), pl.BlockSpec((1, n_e), lambda i: (0, 0))],
        out_shape=[jax.ShapeDtypeStruct((t, d), F32), jax.ShapeDtypeStruct((t, INFO_COLS), F32),
                   jax.ShapeDtypeStruct((1, n_e), F32)],
        scratch_shapes=[pltpu.VMEM((tm, d), F32), pltpu.VMEM((tm, tm), BF16), pltpu.VMEM((1, n_e), F32)],
        compiler_params=_cparams(1),
        name="router",
    )(a, wo, h, mod, g, wr)


EXPERT_TILE_ROWS = 512
DMA_ISSUE_UNROLL = 8


def _row_copy(src_ref, src_row, dst_ref, dst_row, sem):
    return pltpu.make_async_copy(src_ref.at[pl.ds(src_row, 1)], dst_ref.at[pl.ds(dst_row, 1)], sem)


def _dispatch_kernel(slots_ref, ends_ref, x_ref, mod_ref, g_ref, xs_ref, xn_ref, sem, *, geom, n_tiles, rows):
    i = pl.program_id(0)
    tm = x_ref.shape[0]
    n_e = ends_ref.shape[0]

    @pl.when(i == 0)
    def _():
        xn_ref[0:rows, :] = jnp.zeros((rows, xn_ref.shape[1]), xn_ref.dtype)

        def zero_tile(row0):
            cp = pltpu.make_async_copy(xn_ref.at[pl.ds(0, rows)], xs_ref.at[pl.ds(row0, rows)], sem)
            cp.start()
            cp.wait()

        for e in range(n_e):
            start = ends_ref[e - 1] if e else 0
            last_tile = pl.multiple_of(ends_ref[e] - rows, rows)
            pl.when(ends_ref[e] > start)(functools.partial(zero_tile, last_tile))
        for j in range(n_tiles - n_e, n_tiles):
            pl.when(j * rows >= ends_ref[n_e - 1])(functools.partial(zero_tile, j * rows))

    _modulate_tile(x_ref, xn_ref, mod_ref, g_ref, i, 3, geom)

    def body(r, carry):
        for k in range(TOP_K):
            _row_copy(xn_ref, r, xs_ref, slots_ref[TOP_K * r + k], sem).start(priority=k)
        return carry

    lax.fori_loop(0, tm, body, 0, unroll=DMA_ISSUE_UNROLL)
    for k in range(TOP_K):
        pltpu.make_async_copy(xn_ref, xs_ref.at[pl.ds(0, tm)], sem).wait()


def _dispatch(slots, ends, x, mod, g, n_tiles, rows, geom, tm):
    t, d = x.shape
    assert rows <= tm
    return pl.pallas_call(
        functools.partial(_dispatch_kernel, geom=geom, n_tiles=n_tiles, rows=rows),
        grid=(t // tm,),
        in_specs=[pl.BlockSpec((TOP_K * tm,), lambda i: (i,), memory_space=pltpu.SMEM),
                  pl.BlockSpec(memory_space=pltpu.SMEM),
                  pl.BlockSpec((tm, d), lambda i: (i, 0)), _full(mod), _full(g)],
        out_specs=pl.BlockSpec(memory_space=pl.ANY),
        out_shape=jax.ShapeDtypeStruct((n_tiles * rows, d), F32),
        scratch_shapes=[pltpu.VMEM((tm, d), F32), pltpu.SemaphoreType.DMA(())],
        compiler_params=_cparams(1),
        name="moe_dispatch",
    )(slots, ends, x, mod, g)


def _experts_kernel(tile_expert_ref, n_used_ref, xs_ref, w13_ref, w2_ref, y_ref, *, tf):
    del tile_expert_ref
    i = pl.program_id(0)

    @pl.when(i < n_used_ref[0])
    def _():
        y_ref[...] = _swiglu(xs_ref[...].astype(BF16), w13_ref, w2_ref, tf)

    @pl.when(i >= n_used_ref[0])
    def _():
        y_ref[...] = jnp.zeros_like(y_ref)


def _experts(tile_expert, n_used, xs, w13, w2, rows, tf):
    p, d = xs.shape
    _, f_total, _ = w2.shape
    once = pl.Buffered(1)
    grid_spec = pltpu.PrefetchScalarGridSpec(
        num_scalar_prefetch=2,
        grid=(p // rows,),
        in_specs=[pl.BlockSpec((rows, d), lambda i, te, nu: (i, 0)),
                  pl.BlockSpec((None, d, 2 * f_total), lambda i, te, nu: (te[i], 0, 0), pipeline_mode=once),
                  pl.BlockSpec((None, f_total, d), lambda i, te, nu: (te[i], 0, 0), pipeline_mode=once)],
        out_specs=pl.BlockSpec((rows, d), lambda i, te, nu: (i, 0)),
    )
    return pl.pallas_call(
        functools.partial(_experts_kernel, tf=tf),
        grid_spec=grid_spec,
        out_shape=jax.ShapeDtypeStruct((p, d), F32),
        compiler_params=_cparams(1),
        name="moe_experts",
    )(tile_expert, n_used, xs, w13, w2)


def _combine_kernel(slots_ref, x_ref, info_ref, mod_ref, y_ref, *refs, geom):
    fg_ref = refs[0] if len(refs) == 4 else None
    o_ref, yb_ref, sem = refs[-3:]
    i = pl.program_id(0)
    tm, d = x_ref.shape

    def body(r, carry):
        for k in range(TOP_K):
            _row_copy(y_ref, slots_ref[TOP_K * r + k], yb_ref.at[k], r, sem).start(priority=k)
        return carry

    lax.fori_loop(0, tm, body, 0, unroll=DMA_ISSUE_UNROLL)
    for k in range(TOP_K):
        pltpu.make_async_copy(y_ref.at[pl.ds(0, tm)], yb_ref.at[k], sem).wait()

    lane = lax.broadcasted_iota(jnp.int32, (SEG, INFO_COLS), 1)
    for s in range(geom["segs_per_tile"]):
        row = _mod_row(i, s, **geom)
        gate2 = _mod_vec(mod_ref, row, 5, d)
        sl = slice(s * SEG, (s + 1) * SEG)
        info = info_ref[sl, :]
        w1 = jnp.sum(jnp.where(lane == 2, info, 0.0), axis=1, keepdims=True)
        w2 = jnp.sum(jnp.where(lane == 3, info, 0.0), axis=1, keepdims=True)
        out = x_ref[sl, :] + gate2 * (w1 * yb_ref[0, sl, :] + w2 * yb_ref[1, sl, :])
        o_ref[sl, :] = out if fg_ref is None else _rms(out) * fg_ref[...]


def _combine(slots, x, info, mod, y, geom, tm, final=None):
    t, d = x.shape
    n_rows = final[1] if final else t
    extra = [final[0]] if final else []
    return pl.pallas_call(
        functools.partial(_combine_kernel, geom=geom),
        grid=(n_rows // tm,),
        in_specs=[pl.BlockSpec((TOP_K * tm,), lambda i: (i,), memory_space=pltpu.SMEM),
                  pl.BlockSpec((tm, d), lambda i: (i, 0)),
                  pl.BlockSpec((tm, INFO_COLS), lambda i: (i, 0)), _full(mod),
                  pl.BlockSpec(memory_space=pl.ANY)] + [_full(a) for a in extra],
        out_specs=pl.BlockSpec((tm, d), lambda i: (i, 0)),
        out_shape=jax.ShapeDtypeStruct((n_rows, d), F32),
        scratch_shapes=[pltpu.VMEM((TOP_K, tm, d), F32), pltpu.SemaphoreType.DMA(())],
        compiler_params=_cparams(1),
        name="moe_combine",
    )(slots, x, info, mod, y, *extra)


def _mix_moe(a, wo, h, mod, g, wr, w13, w2, geom, tm, tf, final=None):
    t, d = h.shape
    n_e = wr.shape[1]
    x, info, counts = _router(a, wo, h, mod, g, wr, geom, tm)
    rows = EXPERT_TILE_ROWS
    counts = counts[0].astype(jnp.int32)
    padded = (counts + rows - 1) // rows * rows
    ends = jnp.cumsum(padded)
    starts = ends - padded
    n_tiles = TOP_K * t // rows + n_e
    tile_start = jnp.arange(n_tiles, dtype=jnp.int32) * rows
    tile_expert = jnp.minimum(jnp.sum((ends[None, :] <= tile_start[:, None]).astype(jnp.int32), axis=1), n_e - 1)
    n_used = (ends[-1:] // rows).astype(jnp.int32)
    experts = info[:, :TOP_K].astype(jnp.int32)
    ranks = info[:, 4:4 + TOP_K].astype(jnp.int32)
    slots = (starts[experts] + ranks).reshape(-1)
    xs = _dispatch(slots, ends, x, mod, g, n_tiles, rows, geom, tm)
    y = _experts(tile_expert, n_used, xs, w13, w2, rows, tf)
    return _combine(slots, x, info, mod, y, geom, tm, final)


def _proj_b_kernel(x_ref, mod_ref, g_ref, wd_ref, qg_ref, kvg_ref, wq_ref, wk_ref, wv_ref,
                   cos_ref, sa_ref, sb_ref, q_ref, k_ref, v_ref, xn_ref, *, geom, scale):
    i = pl.program_id(0)
    _modulate_tile(x_ref, xn_ref, mod_ref, g_ref, i, 0, geom)
    xn = xn_ref[...]
    cos, sa, sb = cos_ref[...], sa_ref[...], sb_ref[...]
    dq = jnp.dot(xn, wd_ref[:, :B_Q_LORA], preferred_element_type=F32)
    dkv = jnp.dot(xn, wd_ref[:, B_Q_LORA:B_Q_LORA + B_KV_LORA], preferred_element_type=F32)
    kr = jnp.dot(xn, wd_ref[:, B_Q_LORA + B_KV_LORA:], preferred_element_type=F32)
    dqn = (_rms(dq) * qg_ref[...]).astype(BF16)
    dkvn = (_rms(dkv) * kvg_ref[...]).astype(BF16)
    kr = _rope(kr, cos, sa, sb, B_ROPE_DIM // 4).astype(k_ref.dtype)
    for h in range(B_HEADS):
        qh = jnp.dot(dqn, wq_ref[:, h * MXU_N:(h + 1) * MXU_N], preferred_element_type=F32)
        q_ref[:, h * MXU_N:h * MXU_N + LANES] = (qh[:, :LANES] * scale).astype(q_ref.dtype)
        q_ref[:, h * MXU_N + LANES:(h + 1) * MXU_N] = (
            _rope(qh[:, LANES:], cos, sa, sb, B_ROPE_DIM // 4) * scale).astype(q_ref.dtype)
        k_ref[:, h * MXU_N + LANES:(h + 1) * MXU_N] = kr
    for pair in range(B_HEADS // 2):
        sl = slice(pair * MXU_N, (pair + 1) * MXU_N)
        kn = jnp.dot(dkvn, wk_ref[:, sl], preferred_element_type=F32)
        for half in range(2):
            h = 2 * pair + half
            k_ref[:, h * MXU_N:h * MXU_N + LANES] = kn[:, half * LANES:(half + 1) * LANES].astype(k_ref.dtype)
        v_ref[:, sl] = jnp.dot(dkvn, wv_ref[:, sl], preferred_element_type=F32).astype(v_ref.dtype)


def _proj_b(x, mod, g, wd, qg, kvg, wq, wk, wv, tabs, geom, tm):
    t, d = x.shape
    rows = lambda width: pl.BlockSpec((tm, width), lambda i: (i, 0))
    nq, nv = B_HEADS * MXU_N, B_HEADS * B_V_DIM
    return pl.pallas_call(
        functools.partial(_proj_b_kernel, geom=geom, scale=(B_NOPE_DIM + B_ROPE_DIM) ** -0.5 * LOG2E),
        grid=(t // tm,),
        in_specs=[rows(d), _full(mod), _full(g), _full(wd), _full(qg), _full(kvg), _full(wq), _full(wk),
                  _full(wv), rows(LANES), rows(LANES), rows(LANES)],
        out_specs=[rows(nq), rows(nq), rows(nv)],
        out_shape=[jax.ShapeDtypeStruct((t, nq), BF16), jax.ShapeDtypeStruct((t, nq), BF16),
                   jax.ShapeDtypeStruct((t, nv), BF16)],
        scratch_shapes=[pltpu.VMEM((tm, d), BF16)],
        compiler_params=_cparams(1),
        name="proj_b",
    )(x, mod, g, wd, qg, kvg, wq, wk, wv, *tabs)


def _gelu(x):
    return 0.5 * x * (1.0 + lax.erf(x * (2.0 ** -0.5)))


def _sgu_kernel(x_ref, mod_ref, g_ref, win_ref, lng_ref, lnb_ref, ws_ref, bs_ref, gated_ref,
                xn_ref, u_ref, v_ref, *, geom):
    i = pl.program_id(0)
    tm = x_ref.shape[0]
    cw = u_ref.shape[1]
    _modulate_tile(x_ref, xn_ref, mod_ref, g_ref, i, 0, geom)
    xn = xn_ref[...]
    for j in range(cw // MXU_N):
        sl = slice(j * MXU_N, (j + 1) * MXU_N)
        u_ref[:, sl] = _gelu(jnp.dot(xn, win_ref[:, sl], preferred_element_type=F32))
        v_ref[:, sl] = _gelu(jnp.dot(xn, win_ref[:, cw + j * MXU_N:cw + (j + 1) * MXU_N],
                                     preferred_element_type=F32))
    lng, lnb = lng_ref[...], lnb_ref[...]
    gw = cw // C_GROUPS
    for c in range(tm // C_CHUNK):
        rs = slice(c * C_CHUNK, (c + 1) * C_CHUNK)
        v = v_ref[rs, :]
        mu = _row_sum(v) * (1.0 / cw)
        vc = v - mu
        var = _row_sum(vc * vc) * (1.0 / cw)
        vn = (vc * lax.rsqrt(var + EPS) * lng + lnb).astype(BF16)
        for gi in range(C_GROUPS):
            cs = slice(gi * gw, (gi + 1) * gw)
            mixed = jnp.dot(ws_ref[gi], vn[:, cs], preferred_element_type=F32) + bs_ref[:, cs]
            gated_ref[rs, cs] = (u_ref[rs, cs] * mixed).astype(gated_ref.dtype)


def _sgu(x, mod, g, win, lng, lnb, ws, bs, geom, tm):
    t, d = x.shape
    cw = win.shape[1] // 2
    return pl.pallas_call(
        functools.partial(_sgu_kernel, geom=geom),
        grid=(t // tm,),
        in_specs=[pl.BlockSpec((tm, d), lambda i: (i, 0)), _full(mod), _full(g), _full(win), _full(lng),
                  _full(lnb), _full(ws), _full(bs)],
        out_specs=pl.BlockSpec((tm, cw), lambda i: (i, 0)),
        out_shape=jax.ShapeDtypeStruct((t, cw), BF16),
        scratch_shapes=[pltpu.VMEM((tm, d), BF16), pltpu.VMEM((tm, cw), F32), pltpu.VMEM((tm, cw), F32)],
        compiler_params=_cparams(1),
        name="sgu",
    )(x, mod, g, win, lng, lnb, ws, bs)


def _proj_d_kernel(x_ref, mod_ref, g_ref, w_ref, cos_ref, sa_ref, sb_ref, q_ref, k_ref, v_ref, xn_ref,
                   *, geom, scale):
    i = pl.program_id(0)
    _modulate_tile(x_ref, xn_ref, mod_ref, g_ref, i, 0, geom)
    xn = xn_ref[...]
    cos, sa, sb = cos_ref[...], sa_ref[...], sb_ref[...]
    nq = q_ref.shape[1]
    for pair in range(nq // MXU_N):
        y2 = jnp.dot(xn, w_ref[:, pair * MXU_N:(pair + 1) * MXU_N], preferred_element_type=F32)
        for half in range(2):
            sl = slice(half * LANES, (half + 1) * LANES)
            y = _rope(y2[:, sl], cos, sa, sb, D_HEAD_DIM // 4) * scale
            q_ref[:, pair * MXU_N + half * LANES:pair * MXU_N + (half + 1) * LANES] = y.astype(q_ref.dtype)
    kv = jnp.dot(xn, w_ref[:, nq:nq + MXU_N], preferred_element_type=F32)
    k_ref[...] = _rope(kv[:, :LANES], cos, sa, sb, D_HEAD_DIM // 4).astype(k_ref.dtype)
    v_ref[...] = kv[:, LANES:].astype(v_ref.dtype)


def _proj_d(x, mod, g, w, tabs, geom, tm):
    t, d = x.shape
    nq = D_HEADS * D_HEAD_DIM
    rows = lambda width: pl.BlockSpec((tm, width), lambda i: (i, 0))
    return pl.pallas_call(
        functools.partial(_proj_d_kernel, geom=geom, scale=D_HEAD_DIM ** -0.5 * LOG2E),
        grid=(t // tm,),
        in_specs=[rows(d), _full(mod), _full(g), _full(w), rows(LANES), rows(LANES), rows(LANES)],
        out_specs=[rows(nq), rows(LANES), rows(LANES)],
        out_shape=[jax.ShapeDtypeStruct((t, nq), BF16), jax.ShapeDtypeStruct((t, LANES), BF16),
                   jax.ShapeDtypeStruct((t, LANES), BF16)],
        scratch_shapes=[pltpu.VMEM((tm, d), BF16)],
        compiler_params=_cparams(1),
        name="proj_d",
    )(x, mod, g, w, *tabs)


WIN_STACK = 4


def _win_kernel(sink_ref, q_ref, kc_ref, km_ref, k0_ref, kp_ref, vc_ref, vm_ref, v0_ref, vp_ref, o_ref,
                *, lat_blocks, blocks_per_sample):
    j = pl.program_id(0)
    blk = q_ref.shape[0]
    ctx = kc_ref.shape[0]
    kk = jnp.concatenate([kc_ref[...], km_ref[...], k0_ref[...], kp_ref[...]], axis=0)
    vv = jnp.concatenate([vc_ref[...], vm_ref[...], v0_ref[...], vp_ref[...]], axis=0)
    n_keys = kk.shape[0]
    n = lax.rem(j, jnp.int32(blocks_per_sample))
    qpos = lax.broadcasted_iota(jnp.int32, (blk, n_keys), 0)
    col = lax.broadcasted_iota(jnp.int32, (blk, n_keys), 1)
    rel = col - ctx - blk - qpos
    kblock = n - 1 + lax.shift_right_arithmetic(col - ctx, int(math.log2(blk)))
    far = jnp.where(j < lat_blocks, 0, 2 * D_WINDOW + 2)
    in_window = (jnp.abs(rel) + far <= D_WINDOW) & (kblock >= 0) & (kblock < blocks_per_sample)
    mask = (col < ctx) | in_window
    group = D_HEADS // D_KV_HEADS
    bias = jnp.concatenate([jnp.where(mask, 0.0, MASK_VALUE)] * WIN_STACK, axis=0)
    for h0 in range(0, D_HEADS, WIN_STACK):
        hk = h0 // group
        heads = range(h0, h0 + WIN_STACK)
        q = jnp.concatenate([q_ref[:, h * D_HEAD_DIM:(h + 1) * D_HEAD_DIM] for h in heads], axis=0)
        sink = jnp.concatenate([jnp.full((blk, 1), sink_ref[h] * LOG2E, F32) for h in heads], axis=0)
        k = kk[:, hk * D_HEAD_DIM:(hk + 1) * D_HEAD_DIM]
        v = vv[:, hk * D_HEAD_DIM:(hk + 1) * D_HEAD_DIM]
        v_ones = jnp.concatenate([v, jnp.ones_like(v)], axis=1)
        s = lax.dot_general(q, k, (((1,), (1,)), ((), ())), preferred_element_type=F32) + bias
        m = jnp.maximum(_row_max(s), sink)
        e = jnp.exp2((s - m).astype(v.dtype))
        pv = jnp.dot(e, v_ones, preferred_element_type=F32)
        o = pv[:, :D_HEAD_DIM] / (pv[:, D_HEAD_DIM:] + jnp.exp2(sink - m))
        for n_h, h in enumerate(heads):
            o_ref[:, h * D_HEAD_DIM:(h + 1) * D_HEAD_DIM] = o[n_h * blk:(n_h + 1) * blk, :].astype(o_ref.dtype)


def _win_attn(sinks, q, k, v, *, n_batch, lat, ctx, blk=128):
    t, nq = q.shape
    bps = lat // blk
    lat_blocks = n_batch * bps
    ctx_bps = ctx // blk
    ctx_blk0 = n_batch * lat // ctx

    def sample(j):
        return jnp.where(j < lat_blocks, j // bps, (j - lat_blocks) // ctx_bps)

    def neighbour(j, off):
        b = sample(j)
        n = jnp.where(j < lat_blocks, j - b * bps, 0)
        return b * bps + jnp.clip(n + off, 0, bps - 1)

    kv_specs = [pl.BlockSpec((ctx, LANES), lambda j: (ctx_blk0 + sample(j), 0)),
                pl.BlockSpec((blk, LANES), lambda j: (neighbour(j, -1), 0)),
                pl.BlockSpec((blk, LANES), lambda j: (neighbour(j, 0), 0)),
                pl.BlockSpec((blk, LANES), lambda j: (neighbour(j, 1), 0))]
    return pl.pallas_call(
        functools.partial(_win_kernel, lat_blocks=lat_blocks, blocks_per_sample=bps),
        grid=(t // blk,),
        in_specs=[pl.BlockSpec(memory_space=pltpu.SMEM),
                  pl.BlockSpec((blk, nq), lambda j: (j, 0))] + kv_specs + kv_specs,
        out_specs=pl.BlockSpec((blk, nq), lambda j: (j, 0)),
        out_shape=jax.ShapeDtypeStruct((t, nq), BF16),
        compiler_params=_cparams(1),
        name="win_attn",
    )(sinks, q, k, k, k, k, v, v, v, v)


def _rope_tables(n_batch, lat, ctx, dim, tiled):
    quarter = dim // 4
    pos = jnp.arange(lat, dtype=jnp.int32)
    rows, cols = pos // GRID_W, pos % GRID_W
    inv_freq = ROPE_THETA ** (-jnp.arange(quarter, dtype=F32) / quarter)
    ang_r = rows.astype(F32)[:, None] * inv_freq
    ang_c = cols.astype(F32)[:, None] * inv_freq
    ang = jnp.concatenate([ang_r, ang_r, ang_c, ang_c], axis=-1)
    cos, sin = jnp.cos(ang), jnp.sin(ang)
    first = (jnp.arange(dim) % (2 * quarter)) < quarter
    out = []
    for tab, fill in ((cos, 1.0), (jnp.where(first, -sin, 0.0), 0.0), (jnp.where(first, 0.0, sin), 0.0)):
        if tiled:
            tab = jnp.tile(tab, (1, LANES // dim))
        else:
            tab = jnp.concatenate([tab, jnp.full((lat, LANES - dim), fill, F32)], axis=1)
        out.append(jnp.concatenate([jnp.tile(tab, (n_batch, 1)), jnp.full((n_batch * ctx, LANES), fill, F32)],
                                   axis=0))
    return tuple(out)


def _pick_tf(f, cap):
    return max(tf for tf in range(LANES, cap + 1, LANES) if f % tf == 0)


def kernel(x, c, ctx, c_ctx, ada_w, ada_b, norm_mix, norm_ffn, final_norm, a_wqkv, a_q_norm, a_k_norm, a_wo, b_w_down, b_q_lora_norm, b_kv_lora_norm, b_w_uq, b_w_ukv, b_wo, c_w_in, c_ln_g, c_ln_b, c_w_spatial, c_b_spatial, c_w_out, d_wqkv, d_sinks, d_wo, ffn_w13, ffn_w2, moe_router, moe_w13, moe_w2):
    n_batch, lat, d = x.shape
    n_ctx = ctx.shape[1]
    depth = ada_w.shape[0]
    flash_rows = min(FLASH_ROWS, lat)
    assert n_ctx == SEG and lat % flash_rows == 0 and lat % 512 == 0 and n_batch < MOD_ROWS and depth == 4
    ctx_segs = n_batch * n_ctx // SEG
    segs_per_tile = max(k for k in (4, 2, 1) if ctx_segs % k == 0)
    tm = segs_per_tile * SEG
    geom = dict(segs_per_tile=segs_per_tile, lat_segs=n_batch * lat // SEG, segs_per_sample=lat // SEG,
                n_batch=n_batch)
    half_segs = min(2, segs_per_tile)
    geom_half, tm_half = dict(geom, segs_per_tile=half_segs), half_segs * SEG

    cvec = jnp.concatenate([c, c_ctx[None, :], jnp.zeros((MOD_ROWS - n_batch - 1, d), F32)], axis=0)
    mod = _ada_mod(cvec, ada_w, ada_b)
    h = (x.reshape(n_batch * lat, d), ctx.reshape(n_batch * n_ctx, d))

    tabs_a = _rope_tables(n_batch, lat, n_ctx, A_HEAD_DIM, True)
    tabs_d = _rope_tables(n_batch, lat, n_ctx, D_HEAD_DIM, True)
    tabs_b = _rope_tables(n_batch, lat, n_ctx, B_ROPE_DIM, False)
    row = lambda v: v.reshape(1, -1)
    dense_tf = _pick_tf(ffn_w13.shape[-1] // 2, 256)
    moe_tf = _pick_tf(moe_w13.shape[-1] // 2, 512)
    attn = functools.partial(_flash, n_batch=n_batch, lat=lat, ctx=n_ctx)

    qkv = _proj_a(*h, mod[0], row(norm_mix[0]), a_wqkv[0].astype(BF16), row(a_q_norm[0]), row(a_k_norm[0]),
                  tabs_a, geom, tm)
    grp = A_HEADS // A_KV_HEADS
    o = attn(qkv, qkv, qkv, n_groups=A_KV_HEADS, heads=grp, dq=A_HEAD_DIM, dv=A_HEAD_DIM,
             k_col=lambda g: A_HEADS + g, v_col=lambda g: A_HEADS + A_KV_HEADS + g, tq=flash_rows // grp)
    h = _mix_ffn(o, a_wo[0].astype(BF16), h, mod[0], row(norm_ffn[0]), ffn_w13[0].astype(BF16),
                 ffn_w2[0].astype(BF16), geom_half, tm_half, dense_tf)

    wd = b_w_down[0]
    wd = jnp.concatenate([wd, jnp.zeros((d, LANES - B_ROPE_DIM), F32)], axis=1).astype(BF16)
    wq = b_w_uq[0].reshape(B_Q_LORA, B_HEADS, B_NOPE_DIM + B_ROPE_DIM)
    wq = jnp.concatenate([wq, jnp.zeros((B_Q_LORA, B_HEADS, MXU_N - B_NOPE_DIM - B_ROPE_DIM), F32)], axis=2)
    wq = wq.reshape(B_Q_LORA, B_HEADS * MXU_N).astype(BF16)
    wkv = b_w_ukv[0].reshape(B_KV_LORA, B_HEADS, B_NOPE_DIM + B_V_DIM)
    wk = wkv[:, :, :B_NOPE_DIM].reshape(B_KV_LORA, B_HEADS * B_NOPE_DIM).astype(BF16)
    wv = wkv[:, :, B_NOPE_DIM:].reshape(B_KV_LORA, B_HEADS * B_V_DIM).astype(BF16)
    qc, kc, vb = _proj_b(h, mod[1], row(norm_mix[1]), wd, row(b_q_lora_norm[0]), row(b_kv_lora_norm[0]),
                         wq, wk, wv, tabs_b, geom, tm)
    o = attn(qc, kc, vb, n_groups=B_HEADS, heads=1, dq=MXU_N, dv=B_V_DIM,
             k_col=lambda g: g, v_col=lambda g: g, tq=flash_rows)
    h = _mix_moe(o, b_wo[0].astype(BF16), h, mod[1], row(norm_ffn[1]), moe_router[0], moe_w13[0].astype(BF16),
                 moe_w2[0].astype(BF16), geom, tm, moe_tf)

    bs = jnp.repeat(c_b_spatial[0].T, c_w_out.shape[1] // C_GROUPS, axis=1)
    gated = _sgu(h, mod[2], row(norm_mix[2]), c_w_in[0].astype(BF16), row(c_ln_g[0]), row(c_ln_b[0]),
                 c_w_spatial[0].astype(BF16), bs, geom_half, tm_half)
    h = _mix_ffn(gated, c_w_out[0].astype(BF16), h, mod[2], row(norm_ffn[2]), ffn_w13[1].astype(BF16),
                 ffn_w2[1].astype(BF16), geom_half, tm_half, dense_tf)

    q3, k3, v3 = _proj_d(h, mod[3], row(norm_mix[3]), d_wqkv[0].astype(BF16), tabs_d, geom, tm)
    o = _win_attn(d_sinks[0], q3, k3, v3, n_batch=n_batch, lat=lat, ctx=n_ctx)
    out = _mix_moe(o, d_wo[0].astype(BF16), h, mod[3], row(norm_ffn[3]), moe_router[1], moe_w13[1].astype(BF16),
                   moe_w2[1].astype(BF16), geom, tm, moe_tf, final=(row(final_norm), n_batch * lat))
    return out.reshape(n_batch, lat, d)
```

```python
import functools
import math

import jax
import jax.numpy as jnp
from jax import lax
from jax.experimental import pallas as pl
from jax.experimental.pallas import tpu as pltpu

F32 = jnp.float32
BF16 = jnp.bfloat16

EPS = 1e-6
ROPE_THETA = 10000.0
GRID_W = 64
MASK_VALUE = -1e30
LOG2E = math.log2(math.e)

A_HEADS, A_KV_HEADS, A_HEAD_DIM = 8, 2, 128
B_HEADS, B_Q_LORA, B_KV_LORA, B_NOPE_DIM, B_ROPE_DIM, B_V_DIM = 8, 384, 256, 128, 64, 128
C_GROUPS, C_CHUNK = 8, 128
D_HEADS, D_KV_HEADS, D_HEAD_DIM, D_WINDOW = 16, 2, 64, 128
TOP_K = 2

SEG = 256
LANES = 128
MXU_N = 256
VMEM_LIMIT = 56 * 1024 * 1024

MOD_ROWS = 8
INFO_COLS = 8


def _cparams(n_axes):
    return pltpu.CompilerParams(dimension_semantics=("arbitrary",) * n_axes,
                                vmem_limit_bytes=VMEM_LIMIT)


def _full(a):
    return pl.BlockSpec(a.shape, lambda *_: (0,) * a.ndim)


def _mod_row(tile, s, segs_per_tile, lat_segs, segs_per_sample, n_batch):
    g = tile * segs_per_tile + s
    return jnp.where(g < lat_segs, lax.div(g, jnp.int32(segs_per_sample)), n_batch)


def _mod_vec(mod_ref, row, k, d):
    return mod_ref[pl.ds(row, 1), k * d:(k + 1) * d]


def _row_sum(x):
    return jnp.sum(x, axis=-1, keepdims=True)


def _row_max(x):
    return jnp.max(x, axis=-1, keepdims=True)


def _rms(x):
    return x * lax.rsqrt(jnp.mean(x * x, axis=-1, keepdims=True) + EPS)


def _modulate_tile(x_ref, xn_ref, mod_ref, g_ref, tile, k_shift, geom):
    d = x_ref.shape[-1]
    g = g_ref[...]
    for s in range(geom["segs_per_tile"]):
        row = _mod_row(tile, s, **geom)
        shift = _mod_vec(mod_ref, row, k_shift, d)
        scale = _mod_vec(mod_ref, row, k_shift + 1, d)
        x = x_ref[s * SEG:(s + 1) * SEG, :]
        y = _rms(x) * g * (1.0 + scale) + shift
        xn_ref[s * SEG:(s + 1) * SEG, :] = y.astype(xn_ref.dtype)


def _rope(t, cos, sin_a, sin_b, shift):
    return (t * cos + pltpu.roll(t, LANES - shift, axis=1) * sin_a
            + pltpu.roll(t, shift, axis=1) * sin_b)


def _ada_kernel(c_ref, w_ref, b_ref, o_ref):
    c = c_ref[...]
    a = c * jax.nn.sigmoid(c)
    o_ref[0] = jnp.dot(a, w_ref[0], preferred_element_type=F32,
                       precision=lax.Precision.HIGHEST) + b_ref[0]


def _ada_mod(cvec, ada_w, ada_b):
    depth, d, m = ada_w.shape
    tn = m // 4
    return pl.pallas_call(
        _ada_kernel,
        grid=(depth, m // tn),
        in_specs=[pl.BlockSpec((MOD_ROWS, d), lambda l, j: (0, 0)),
                  pl.BlockSpec((1, d, tn), lambda l, j: (l, 0, j)),
                  pl.BlockSpec((1, 1, tn), lambda l, j: (l, 0, j))],
        out_specs=pl.BlockSpec((1, MOD_ROWS, tn), lambda l, j: (l, 0, j)),
        out_shape=jax.ShapeDtypeStruct((depth, MOD_ROWS, m), F32),
        compiler_params=_cparams(2),
        name="ada_mod",
    )(cvec, ada_w, ada_b.reshape(depth, 1, m))


def _stream_tile(xl_ref, xc_ref, buf_ref, i, n_lat_tiles):
    @pl.when(i < n_lat_tiles)
    def _():
        buf_ref[...] = xl_ref[...]

    @pl.when(i >= n_lat_tiles)
    def _():
        buf_ref[...] = xc_ref[...]


def _stream_specs(tm, d, n_lat_tiles):
    return [pl.BlockSpec((tm, d), lambda i: (jnp.minimum(i, n_lat_tiles - 1), 0)),
            pl.BlockSpec((tm, d), lambda i: (jnp.maximum(i - n_lat_tiles, 0), 0))]


def _proj_a_kernel(xl_ref, xc_ref, mod_ref, g_ref, w_ref, qn_ref, kn_ref, cos_ref, sa_ref, sb_ref, o_ref,
                   x_ref, xn_ref, *, geom, scale, n_lat_tiles):
    i = pl.program_id(0)
    _stream_tile(xl_ref, xc_ref, x_ref, i, n_lat_tiles)
    _modulate_tile(x_ref, xn_ref, mod_ref, g_ref, i, 0, geom)
    xn = xn_ref[...]
    cos, sa, sb = cos_ref[...], sa_ref[...], sb_ref[...]
    qn, kn = qn_ref[...], kn_ref[...]
    n_q, n_k = A_HEADS, A_KV_HEADS
    n_heads = n_q + 2 * n_k
    for pair in range(n_heads // 2):
        y2 = jnp.dot(xn, w_ref[:, pair * MXU_N:(pair + 1) * MXU_N], preferred_element_type=F32)
        for half in range(2):
            h = 2 * pair + half
            y = y2[:, half * LANES:(half + 1) * LANES]
            if h < n_q:
                y = _rope(_rms(y) * qn, cos, sa, sb, A_HEAD_DIM // 4) * scale
            elif h < n_q + n_k:
                y = _rope(_rms(y) * kn, cos, sa, sb, A_HEAD_DIM // 4)
            o_ref[:, h * LANES:(h + 1) * LANES] = y.astype(o_ref.dtype)


def _proj_a(xl, xc, mod, g, w, qn, kn, tabs, geom, tm):
    d = xl.shape[1]
    t = xl.shape[0] + xc.shape[0]
    n = w.shape[1]
    n_lat_tiles = xl.shape[0] // tm
    rows = lambda width: pl.BlockSpec((tm, width), lambda i: (i, 0))
    return pl.pallas_call(
        functools.partial(_proj_a_kernel, geom=geom, scale=A_HEAD_DIM ** -0.5 * LOG2E, n_lat_tiles=n_lat_tiles),
        grid=(t // tm,),
        in_specs=_stream_specs(tm, d, n_lat_tiles) + [_full(mod), _full(g), _full(w), _full(qn), _full(kn),
                                                       rows(LANES), rows(LANES), rows(LANES)],
        out_specs=rows(n),
        out_shape=jax.ShapeDtypeStruct((t, n), BF16),
        scratch_shapes=[pltpu.VMEM((tm, d), F32), pltpu.VMEM((tm, d), BF16)],
        compiler_params=_cparams(1),
        name="proj_a",
    )(xl, xc, mod, g, w, qn, kn, *tabs)


FLASH_ROWS = 2048
FLASH_UNROLL = 16


def _flash_kernel(*refs, heads, dq, dv, tk, n_lat_chunks):
    if n_lat_chunks:
        _, q_ref, kc_ref, vc_ref, kl_ref, vl_ref, o_ref = refs
    else:
        _, q_ref, kc_ref, vc_ref, o_ref = refs
    tq = q_ref.shape[0]
    if heads > 1:
        q = jnp.concatenate([q_ref[:, g * dq:(g + 1) * dq] for g in range(heads)], axis=0)
    else:
        q = q_ref[...]
    rows = heads * tq

    def scores(k):
        return lax.dot_general(q, k, (((1,), (1,)), ((), ())), preferred_element_type=F32)

    def update(s, v, m, acc):
        m_new = jnp.maximum(m, _row_max(s))
        alpha = jnp.exp2(m - m_new)
        p = jnp.exp2((s - m_new).astype(v.dtype))
        v_ones = jnp.concatenate([v, jnp.ones_like(v)], axis=1)
        acc = alpha * acc + jnp.dot(p, v_ones, preferred_element_type=F32)
        return m_new, acc

    m0 = jnp.full((rows, 1), MASK_VALUE, F32)
    acc0 = jnp.zeros((rows, 2 * dv), F32)
    m, acc = update(scores(kc_ref[...]), vc_ref[...], m0, acc0)
    if n_lat_chunks:
        def body(c, carry):
            off = pl.multiple_of(c * tk, tk)
            return update(scores(kl_ref[pl.ds(off, tk), :]), vl_ref[pl.ds(off, tk), :], *carry)

        m, acc = lax.fori_loop(0, n_lat_chunks, body, (m, acc), unroll=FLASH_UNROLL)
    o = acc[:, :dv] / acc[:, dv:]
    for g in range(heads):
        o_ref[:, g * dv:(g + 1) * dv] = o[g * tq:(g + 1) * tq, :].astype(o_ref.dtype)


def _flash(q, k, v, *, n_batch, lat, ctx, n_groups, heads, dq, dv, k_col, v_col, tq, tk=512):
    t = q.shape[0]
    nq = lat // tq
    ctx_blk0 = n_batch * lat // ctx
    kern = functools.partial(_flash_kernel, heads=heads, dq=dq, dv=dv, tk=tk)
    out_shape = jax.ShapeDtypeStruct((t, n_groups * heads * dv), BF16)
    o = pl.pallas_call(
        functools.partial(kern, n_lat_chunks=lat // tk),
        grid=(n_batch, n_groups, nq),
        in_specs=[pl.BlockSpec(memory_space=pl.ANY),
                  pl.BlockSpec((tq, heads * dq), lambda b, g, i: (b * nq + i, g)),
                  pl.BlockSpec((ctx, dq), lambda b, g, i: (ctx_blk0 + b, k_col(g))),
                  pl.BlockSpec((ctx, dv), lambda b, g, i: (ctx_blk0 + b, v_col(g))),
                  pl.BlockSpec((lat, dq), lambda b, g, i: (b, k_col(g))),
                  pl.BlockSpec((lat, dv), lambda b, g, i: (b, v_col(g)))],
        out_specs=pl.BlockSpec((tq, heads * dv), lambda b, g, i: (b * nq + i, g)),
        out_shape=out_shape,
        input_output_aliases={0: 0},
        compiler_params=_cparams(3),
        name="flash_lat",
    )(jnp.zeros(out_shape.shape, out_shape.dtype), q, k, v, k, v)
    return pl.pallas_call(
        functools.partial(kern, n_lat_chunks=0),
        grid=(n_batch, n_groups),
        in_specs=[pl.BlockSpec(memory_space=pl.ANY),
                  pl.BlockSpec((ctx, heads * dq), lambda b, g: (ctx_blk0 + b, g)),
                  pl.BlockSpec((ctx, dq), lambda b, g: (ctx_blk0 + b, k_col(g))),
                  pl.BlockSpec((ctx, dv), lambda b, g: (ctx_blk0 + b, v_col(g)))],
        out_specs=pl.BlockSpec((ctx, heads * dv), lambda b, g: (ctx_blk0 + b, g)),
        out_shape=out_shape,
        input_output_aliases={0: 0},
        compiler_params=_cparams(2),
        name="flash_ctx",
    )(o, q, k, v)


def _swiglu(xn, w13_ref, w2_ref, tf):
    f_total = w2_ref.shape[0]
    acc = None
    for f in range(f_total // tf):
        a = jnp.dot(xn, w13_ref[:, f * tf:(f + 1) * tf], preferred_element_type=F32)
        b = jnp.dot(xn, w13_ref[:, f_total + f * tf:f_total + (f + 1) * tf], preferred_element_type=F32)
        y = (a * jax.nn.sigmoid(a) * b).astype(BF16)
        part = jnp.dot(y, w2_ref[f * tf:(f + 1) * tf, :], preferred_element_type=F32)
        acc = part if acc is None else acc + part
    return acc


def _mix_ffn_kernel(a_ref, wo_ref, *refs, geom, tf, n_lat_tiles):
    i = pl.program_id(0)
    if n_lat_tiles is None:
        h_ref, mod_ref, g_ref, w13_ref, w2_ref, o_ref, h1_ref, xn_ref = refs
        h1_ref[...] = h_ref[...]
    else:
        hl_ref, hc_ref, mod_ref, g_ref, w13_ref, w2_ref, o_ref, h1_ref, xn_ref = refs
        _stream_tile(hl_ref, hc_ref, h1_ref, i, n_lat_tiles)
    d = h1_ref.shape[-1]
    segs = [slice(s * SEG, (s + 1) * SEG) for s in range(geom["segs_per_tile"])]
    rows = [_mod_row(i, s, **geom) for s in range(geom["segs_per_tile"])]
    for sl, row in zip(segs, rows):
        y = jnp.dot(a_ref[sl, :], wo_ref[...], preferred_element_type=F32)
        h1_ref[sl, :] += _mod_vec(mod_ref, row, 2, d) * y
    _modulate_tile(h1_ref, xn_ref, mod_ref, g_ref, i, 3, geom)
    acc = _swiglu(xn_ref[...], w13_ref, w2_ref, tf)
    for sl, row in zip(segs, rows):
        o_ref[sl, :] = h1_ref[sl, :] + _mod_vec(mod_ref, row, 5, d) * acc[sl, :]


def _resident(a):
    return pl.BlockSpec(a.shape, lambda *_: (0,) * a.ndim, pipeline_mode=pl.Buffered(1))


def _mix_ffn(a, wo, h, mod, g, w13, w2, geom, tm, tf):
    t, ka = a.shape
    two = isinstance(h, tuple)
    d = (h[0] if two else h).shape[1]
    n_lat_tiles = h[0].shape[0] // tm if two else None
    h_specs = _stream_specs(tm, d, n_lat_tiles) if two else [pl.BlockSpec((tm, d), lambda i: (i, 0))]
    return pl.pallas_call(
        functools.partial(_mix_ffn_kernel, geom=geom, tf=tf, n_lat_tiles=n_lat_tiles),
        grid=(t // tm,),
        in_specs=[pl.BlockSpec((tm, ka), lambda i: (i, 0)), _resident(wo)] + h_specs
        + [_full(mod), _full(g), _resident(w13), _resident(w2)],
        out_specs=pl.BlockSpec((tm, d), lambda i: (i, 0)),
        out_shape=jax.ShapeDtypeStruct((t, d), F32),
        scratch_shapes=[pltpu.VMEM((tm, d), F32), pltpu.VMEM((tm, d), BF16)],
        compiler_params=_cparams(1),
        name="mix_ffn",
    )(a, wo, *(h if two else (h,)), mod, g, w13, w2)


def _router_kernel(a_ref, wo_ref, h_ref, mod_ref, g_ref, wr_ref, x_ref, info_ref, cnt_ref, xn_ref, tri_ref,
                   run_ref, *, geom):
    i = pl.program_id(0)
    tm, d = x_ref.shape

    @pl.when(i == 0)
    def _():
        r = lax.broadcasted_iota(jnp.int32, (tm, tm), 0)
        c = lax.broadcasted_iota(jnp.int32, (tm, tm), 1)
        tri_ref[...] = jnp.where(c < r, 1.0, 0.0).astype(tri_ref.dtype)
        run_ref[...] = jnp.zeros_like(run_ref)

    for s in range(geom["segs_per_tile"]):
        gate = _mod_vec(mod_ref, _mod_row(i, s, **geom), 2, d)
        sl = slice(s * SEG, (s + 1) * SEG)
        x_ref[sl, :] = h_ref[sl, :] + gate * jnp.dot(a_ref[sl, :], wo_ref[...], preferred_element_type=F32)
    _modulate_tile(x_ref, xn_ref, mod_ref, g_ref, i, 3, geom)
    logits = jnp.dot(xn_ref[...], wr_ref[...], preferred_element_type=F32, precision=lax.Precision.HIGHEST)
    n_e = float(logits.shape[1])
    lane = lax.broadcasted_iota(jnp.int32, logits.shape, 1).astype(F32)
    m1 = jnp.max(logits, axis=1, keepdims=True)
    i1 = jnp.min(jnp.where(logits == m1, lane, n_e), axis=1, keepdims=True)
    rest = jnp.where(lane == i1, -jnp.inf, logits)
    m2 = jnp.max(rest, axis=1, keepdims=True)
    i2 = jnp.min(jnp.where(rest == m2, lane, n_e), axis=1, keepdims=True)
    e2 = jnp.exp(m2 - m1)
    w1 = 1.0 / (1.0 + e2)
    w2 = e2 / (1.0 + e2)
    onehot = jnp.where((lane == i1) | (lane == i2), 1.0, 0.0)
    before = jnp.dot(tri_ref[...], onehot.astype(tri_ref.dtype), preferred_element_type=F32) + run_ref[...]
    r1 = jnp.sum(jnp.where(lane == i1, before, 0.0), axis=1, keepdims=True)
    r2 = jnp.sum(jnp.where(lane == i2, before, 0.0), axis=1, keepdims=True)
    run_ref[...] += jnp.sum(onehot, axis=0, keepdims=True)
    cnt_ref[...] = run_ref[...]
    vals = (i1, i2, w1, w2, r1, r2)
    info = jnp.zeros_like(logits)
    for k, val in enumerate(vals):
        info = jnp.where(lane == float(k), val, info)
    info_ref[...] = info


def _router(a, wo, h, mod, g, wr, geom, tm):
    t, d = h.shape
    ka = a.shape[1]
    n_e = wr.shape[1]
    assert n_e == INFO_COLS
    rows = lambda width: pl.BlockSpec((tm, width), lambda i: (i, 0))
    return pl.pallas_call(
        functools.partial(_router_kernel, geom=geom),
        grid=(t // tm,),
        in_specs=[rows(ka), _full(wo), rows(d), _full(mod), _full(g), _full(wr)],
        out_specs=[rows(d), rows(INFO_COLS), pl.BlockSpec((1, n_e), lambda i: (0, 0))],
        out_shape=[jax.ShapeDtypeStruct((t, d), F32), jax.ShapeDtypeStruct((t, INFO_COLS), F32),
                   jax.ShapeDtypeStruct((1, n_e), F32)],
        scratch_shapes=[pltpu.VMEM((tm, d), F32), pltpu.VMEM((tm, tm), BF16), pltpu.VMEM((1, n_e), F32)],
        compiler_params=_cparams(1),
        name="router",
    )(a, wo, h, mod, g, wr)


EXPERT_TILE_ROWS = 512
DMA_ISSUE_UNROLL = 8


def _row_copy(src_ref, src_row, dst_ref, dst_row, sem):
    return pltpu.make_async_copy(src_ref.at[pl.ds(src_row, 1)], dst_ref.at[pl.ds(dst_row, 1)], sem)


def _dispatch_kernel(slots_ref, ends_ref, x_ref, mod_ref, g_ref, xs_ref, xn_ref, sem, *, geom, n_tiles, rows):
    i = pl.program_id(0)
    tm = x_ref.shape[0]
    n_e = ends_ref.shape[0]

    @pl.when(i == 0)
    def _():
        xn_ref[0:rows, :] = jnp.zeros((rows, xn_ref.shape[1]), xn_ref.dtype)

        def zero_tile(row0):
            cp = pltpu.make_async_copy(xn_ref.at[pl.ds(0, rows)], xs_ref.at[pl.ds(row0, rows)], sem)
            cp.start()
            cp.wait()

        for e in range(n_e):
            start = ends_ref[e - 1] if e else 0
            last_tile = pl.multiple_of(ends_ref[e] - rows, rows)
            pl.when(ends_ref[e] > start)(functools.partial(zero_tile, last_tile))
        for j in range(n_tiles - n_e, n_tiles):
            pl.when(j * rows >= ends_ref[n_e - 1])(functools.partial(zero_tile, j * rows))

    _modulate_tile(x_ref, xn_ref, mod_ref, g_ref, i, 3, geom)

    def body(r, carry):
        for k in range(TOP_K):
            _row_copy(xn_ref, r, xs_ref, slots_ref[TOP_K * r + k], sem).start(priority=k)
        return carry

    lax.fori_loop(0, tm, body, 0, unroll=DMA_ISSUE_UNROLL)
    for k in range(TOP_K):
        pltpu.make_async_copy(xn_ref, xs_ref.at[pl.ds(0, tm)], sem).wait()


def _dispatch(slots, ends, x, mod, g, n_tiles, rows, geom, tm):
    t, d = x.shape
    assert rows <= tm
    return pl.pallas_call(
        functools.partial(_dispatch_kernel, geom=geom, n_tiles=n_tiles, rows=rows),
        grid=(t // tm,),
        in_specs=[pl.BlockSpec((TOP_K * tm,), lambda i: (i,), memory_space=pltpu.SMEM),
                  pl.BlockSpec(memory_space=pltpu.SMEM),
                  pl.BlockSpec((tm, d), lambda i: (i, 0)), _full(mod), _full(g)],
        out_specs=pl.BlockSpec(memory_space=pl.ANY),
        out_shape=jax.ShapeDtypeStruct((n_tiles * rows, d), F32),
        scratch_shapes=[pltpu.VMEM((tm, d), F32), pltpu.SemaphoreType.DMA(())],
        compiler_params=_cparams(1),
        name="moe_dispatch",
    )(slots, ends, x, mod, g)


def _experts_kernel(tile_expert_ref, n_used_ref, xs_ref, w13_ref, w2_ref, y_ref, *, tf):
    del tile_expert_ref
    i = pl.program_id(0)

    @pl.when(i < n_used_ref[0])
    def _():
        y_ref[...] = _swiglu(xs_ref[...].astype(BF16), w13_ref, w2_ref, tf)

    @pl.when(i >= n_used_ref[0])
    def _():
        y_ref[...] = jnp.zeros_like(y_ref)


def _experts(tile_expert, n_used, xs, w13, w2, rows, tf):
    p, d = xs.shape
    _, f_total, _ = w2.shape
    once = pl.Buffered(1)
    grid_spec = pltpu.PrefetchScalarGridSpec(
        num_scalar_prefetch=2,
        grid=(p // rows,),
        in_specs=[pl.BlockSpec((rows, d), lambda i, te, nu: (i, 0)),
                  pl.BlockSpec((None, d, 2 * f_total), lambda i, te, nu: (te[i], 0, 0), pipeline_mode=once),
                  pl.BlockSpec((None, f_total, d), lambda i, te, nu: (te[i], 0, 0), pipeline_mode=once)],
        out_specs=pl.BlockSpec((rows, d), lambda i, te, nu: (i, 0)),
    )
    return pl.pallas_call(
        functools.partial(_experts_kernel, tf=tf),
        grid_spec=grid_spec,
        out_shape=jax.ShapeDtypeStruct((p, d), F32),
        compiler_params=_cparams(1),
        name="moe_experts",
    )(tile_expert, n_used, xs, w13, w2)


def _combine_kernel(slots_ref, x_ref, info_ref, mod_ref, y_ref, *refs, geom):
    fg_ref = refs[0] if len(refs) == 4 else None
    o_ref, yb_ref, sem = refs[-3:]
    i = pl.program_id(0)
    tm, d = x_ref.shape

    def body(r, carry):
        for k in range(TOP_K):
            _row_copy(y_ref, slots_ref[TOP_K * r + k], yb_ref.at[k], r, sem).start(priority=k)
        return carry

    lax.fori_loop(0, tm, body, 0, unroll=DMA_ISSUE_UNROLL)
    for k in range(TOP_K):
        pltpu.make_async_copy(y_ref.at[pl.ds(0, tm)], yb_ref.at[k], sem).wait()

    lane = lax.broadcasted_iota(jnp.int32, (SEG, INFO_COLS), 1)
    for s in range(geom["segs_per_tile"]):
        row = _mod_row(i, s, **geom)
        gate2 = _mod_vec(mod_ref, row, 5, d)
        sl = slice(s * SEG, (s + 1) * SEG)
        info = info_ref[sl, :]
        w1 = jnp.sum(jnp.where(lane == 2, info, 0.0), axis=1, keepdims=True)
        w2 = jnp.sum(jnp.where(lane == 3, info, 0.0), axis=1, keepdims=True)
        out = x_ref[sl, :] + gate2 * (w1 * yb_ref[0, sl, :] + w2 * yb_ref[1, sl, :])
        o_ref[sl, :] = out if fg_ref is None else _rms(out) * fg_ref[...]


def _combine(slots, x, info, mod, y, geom, tm, final=None):
    t, d = x.shape
    n_rows = final[1] if final else t
    extra = [final[0]] if final else []
    return pl.pallas_call(
        functools.partial(_combine_kernel, geom=geom),
        grid=(n_rows // tm,),
        in_specs=[pl.BlockSpec((TOP_K * tm,), lambda i: (i,), memory_space=pltpu.SMEM),
                  pl.BlockSpec((tm, d), lambda i: (i, 0)),
                  pl.BlockSpec((tm, INFO_COLS), lambda i: (i, 0)), _full(mod),
                  pl.BlockSpec(memory_space=pl.ANY)] + [_full(a) for a in extra],
        out_specs=pl.BlockSpec((tm, d), lambda i: (i, 0)),
        out_shape=jax.ShapeDtypeStruct((n_rows, d), F32),
        scratch_shapes=[pltpu.VMEM((TOP_K, tm, d), F32), pltpu.SemaphoreType.DMA(())],
        compiler_params=_cparams(1),
        name="moe_combine",
    )(slots, x, info, mod, y, *extra)


def _mix_moe(a, wo, h, mod, g, wr, w13, w2, geom, tm, tf, final=None):
    t, d = h.shape
    n_e = wr.shape[1]
    x, info, counts = _router(a, wo, h, mod, g, wr, geom, tm)
    rows = EXPERT_TILE_ROWS
    counts = counts[0].astype(jnp.int32)
    padded = (counts + rows - 1) // rows * rows
    ends = jnp.cumsum(padded)
    starts = ends - padded
    n_tiles = TOP_K * t // rows + n_e
    tile_start = jnp.arange(n_tiles, dtype=jnp.int32) * rows
    tile_expert = jnp.minimum(jnp.sum((ends[None, :] <= tile_start[:, None]).astype(jnp.int32), axis=1), n_e - 1)
    n_used = (ends[-1:] // rows).astype(jnp.int32)
    experts = info[:, :TOP_K].astype(jnp.int32)
    ranks = info[:, 4:4 + TOP_K].astype(jnp.int32)
    slots = (starts[experts] + ranks).reshape(-1)
    xs = _dispatch(slots, ends, x, mod, g, n_tiles, rows, geom, tm)
    y = _experts(tile_expert, n_used, xs, w13, w2, rows, tf)
    return _combine(slots, x, info, mod, y, geom, tm, final)


def _proj_b_kernel(x_ref, mod_ref, g_ref, wd_ref, qg_ref, kvg_ref, wq_ref, wk_ref, wv_ref,
                   cos_ref, sa_ref, sb_ref, q_ref, k_ref, v_ref, xn_ref, *, geom, scale):
    i = pl.program_id(0)
    _modulate_tile(x_ref, xn_ref, mod_ref, g_ref, i, 0, geom)
    xn = xn_ref[...]
    cos, sa, sb = cos_ref[...], sa_ref[...], sb_ref[...]
    dq = jnp.dot(xn, wd_ref[:, :B_Q_LORA], preferred_element_type=F32)
    dkv = jnp.dot(xn, wd_ref[:, B_Q_LORA:B_Q_LORA + B_KV_LORA], preferred_element_type=F32)
    kr = jnp.dot(xn, wd_ref[:, B_Q_LORA + B_KV_LORA:], preferred_element_type=F32)
    dqn = (_rms(dq) * qg_ref[...]).astype(BF16)
    dkvn = (_rms(dkv) * kvg_ref[...]).astype(BF16)
    kr = _rope(kr, cos, sa, sb, B_ROPE_DIM // 4).astype(k_ref.dtype)
    for h in range(B_HEADS):
        qh = jnp.dot(dqn, wq_ref[:, h * MXU_N:(h + 1) * MXU_N], preferred_element_type=F32)
        q_ref[:, h * MXU_N:h * MXU_N + LANES] = (qh[:, :LANES] * scale).astype(q_ref.dtype)
        q_ref[:, h * MXU_N + LANES:(h + 1) * MXU_N] = (
            _rope(qh[:, LANES:], cos, sa, sb, B_ROPE_DIM // 4) * scale).astype(q_ref.dtype)
        k_ref[:, h * MXU_N + LANES:(h + 1) * MXU_N] = kr
    for pair in range(B_HEADS // 2):
        sl = slice(pair * MXU_N, (pair + 1) * MXU_N)
        kn = jnp.dot(dkvn, wk_ref[:, sl], preferred_element_type=F32)
        for half in range(2):
            h = 2 * pair + half
            k_ref[:, h * MXU_N:h * MXU_N + LANES] = kn[:, half * LANES:(half + 1) * LANES].astype(k_ref.dtype)
        v_ref[:, sl] = jnp.dot(dkvn, wv_ref[:, sl], preferred_element_type=F32).astype(v_ref.dtype)


def _proj_b(x, mod, g, wd, qg, kvg, wq, wk, wv, tabs, geom, tm):
    t, d = x.shape
    rows = lambda width: pl.BlockSpec((tm, width), lambda i: (i, 0))
    nq, nv = B_HEADS * MXU_N, B_HEADS * B_V_DIM
    return pl.pallas_call(
        functools.partial(_proj_b_kernel, geom=geom, scale=(B_NOPE_DIM + B_ROPE_DIM) ** -0.5 * LOG2E),
        grid=(t // tm,),
        in_specs=[rows(d), _full(mod), _full(g), _full(wd), _full(qg), _full(kvg), _full(wq), _full(wk),
                  _full(wv), rows(LANES), rows(LANES), rows(LANES)],
        out_specs=[rows(nq), rows(nq), rows(nv)],
        out_shape=[jax.ShapeDtypeStruct((t, nq), BF16), jax.ShapeDtypeStruct((t, nq), BF16),
                   jax.ShapeDtypeStruct((t, nv), BF16)],
        scratch_shapes=[pltpu.VMEM((tm, d), BF16)],
        compiler_params=_cparams(1),
        name="proj_b",
    )(x, mod, g, wd, qg, kvg, wq, wk, wv, *tabs)


def _gelu(x):
    return 0.5 * x * (1.0 + lax.erf(x * (2.0 ** -0.5)))


def _sgu_kernel(x_ref, mod_ref, g_ref, win_ref, lng_ref, lnb_ref, ws_ref, bs_ref, gated_ref,
                xn_ref, u_ref, v_ref, *, geom):
    i = pl.program_id(0)
    tm = x_ref.shape[0]
    cw = u_ref.shape[1]
    _modulate_tile(x_ref, xn_ref, mod_ref, g_ref, i, 0, geom)
    xn = xn_ref[...]
    for j in range(cw // MXU_N):
        sl = slice(j * MXU_N, (j + 1) * MXU_N)
        u_ref[:, sl] = _gelu(jnp.dot(xn, win_ref[:, sl], preferred_element_type=F32))
        v_ref[:, sl] = _gelu(jnp.dot(xn, win_ref[:, cw + j * MXU_N:cw + (j + 1) * MXU_N],
                                     preferred_element_type=F32))
    lng, lnb = lng_ref[...], lnb_ref[...]
    gw = cw // C_GROUPS
    for c in range(tm // C_CHUNK):
        rs = slice(c * C_CHUNK, (c + 1) * C_CHUNK)
        v = v_ref[rs, :]
        mu = _row_sum(v) * (1.0 / cw)
        vc = v - mu
        var = _row_sum(vc * vc) * (1.0 / cw)
        vn = (vc * lax.rsqrt(var + EPS) * lng + lnb).astype(BF16)
        for gi in range(C_GROUPS):
            cs = slice(gi * gw, (gi + 1) * gw)
            mixed = jnp.dot(ws_ref[gi], vn[:, cs], preferred_element_type=F32) + bs_ref[:, cs]
            gated_ref[rs, cs] = (u_ref[rs, cs] * mixed).astype(gated_ref.dtype)


def _sgu(x, mod, g, win, lng, lnb, ws, bs, geom, tm):
    t, d = x.shape
    cw = win.shape[1] // 2
    return pl.pallas_call(
        functools.partial(_sgu_kernel, geom=geom),
        grid=(t // tm,),
        in_specs=[pl.BlockSpec((tm, d), lambda i: (i, 0)), _full(mod), _full(g), _full(win), _full(lng),
                  _full(lnb), _full(ws), _full(bs)],
        out_specs=pl.BlockSpec((tm, cw), lambda i: (i, 0)),
        out_shape=jax.ShapeDtypeStruct((t, cw), BF16),
        scratch_shapes=[pltpu.VMEM((tm, d), BF16), pltpu.VMEM((tm, cw), F32), pltpu.VMEM((tm, cw), F32)],
        compiler_params=_cparams(1),
        name="sgu",
    )(x, mod, g, win, lng, lnb, ws, bs)


def _proj_d_kernel(x_ref, mod_ref, g_ref, w_ref, cos_ref, sa_ref, sb_ref, q_ref, k_ref, v_ref, xn_ref,
                   *, geom, scale):
    i = pl.program_id(0)
    _modulate_tile(x_ref, xn_ref, mod_ref, g_ref, i, 0, geom)
    xn = xn_ref[...]
    cos, sa, sb = cos_ref[...], sa_ref[...], sb_ref[...]
    nq = q_ref.shape[1]
    for pair in range(nq // MXU_N):
        y2 = jnp.dot(xn, w_ref[:, pair * MXU_N:(pair + 1) * MXU_N], preferred_element_type=F32)
        for half in range(2):
            sl = slice(half * LANES, (half + 1) * LANES)
            y = _rope(y2[:, sl], cos, sa, sb, D_HEAD_DIM // 4) * scale
            q_ref[:, pair * MXU_N + half * LANES:pair * MXU_N + (half + 1) * LANES] = y.astype(q_ref.dtype)
    kv = jnp.dot(xn, w_ref[:, nq:nq + MXU_N], preferred_element_type=F32)
    k_ref[...] = _rope(kv[:, :LANES], cos, sa, sb, D_HEAD_DIM // 4).astype(k_ref.dtype)
    v_ref[...] = kv[:, LANES:].astype(v_ref.dtype)


def _proj_d(x, mod, g, w, tabs, geom, tm):
    t, d = x.shape
    nq = D_HEADS * D_HEAD_DIM
    rows = lambda width: pl.BlockSpec((tm, width), lambda i: (i, 0))
    return pl.pallas_call(
        functools.partial(_proj_d_kernel, geom=geom, scale=D_HEAD_DIM ** -0.5 * LOG2E),
        grid=(t // tm,),
        in_specs=[rows(d), _full(mod), _full(g), _full(w), rows(LANES), rows(LANES), rows(LANES)],
        out_specs=[rows(nq), rows(LANES), rows(LANES)],
        out_shape=[jax.ShapeDtypeStruct((t, nq), BF16), jax.ShapeDtypeStruct((t, LANES), BF16),
                   jax.ShapeDtypeStruct((t, LANES), BF16)],
        scratch_shapes=[pltpu.VMEM((tm, d), BF16)],
        compiler_params=_cparams(1),
        name="proj_d",
    )(x, mod, g, w, *tabs)


WIN_STACK = 4


def _win_kernel(sink_ref, q_ref, kc_ref, km_ref, k0_ref, kp_ref, vc_ref, vm_ref, v0_ref, vp_ref, o_ref,
                *, lat_blocks, blocks_per_sample):
    j = pl.program_id(0)
    blk = q_ref.shape[0]
    ctx = kc_ref.shape[0]
    kk = jnp.concatenate([kc_ref[...], km_ref[...], k0_ref[...], kp_ref[...]], axis=0)
    vv = jnp.concatenate([vc_ref[...], vm_ref[...], v0_ref[...], vp_ref[...]], axis=0)
    n_keys = kk.shape[0]
    n = lax.rem(j, jnp.int32(blocks_per_sample))
    qpos = lax.broadcasted_iota(jnp.int32, (blk, n_keys), 0)
    col = lax.broadcasted_iota(jnp.int32, (blk, n_keys), 1)
    rel = col - ctx - blk - qpos
    kblock = n - 1 + lax.shift_right_arithmetic(col - ctx, int(math.log2(blk)))
    far = jnp.where(j < lat_blocks, 0, 2 * D_WINDOW + 2)
    in_window = (jnp.abs(rel) + far <= D_WINDOW) & (kblock >= 0) & (kblock < blocks_per_sample)
    mask = (col < ctx) | in_window
    group = D_HEADS // D_KV_HEADS
    bias = jnp.concatenate([jnp.where(mask, 0.0, MASK_VALUE)] * WIN_STACK, axis=0)
    for h0 in range(0, D_HEADS, WIN_STACK):
        hk = h0 // group
        heads = range(h0, h0 + WIN_STACK)
        q = jnp.concatenate([q_ref[:, h * D_HEAD_DIM:(h + 1) * D_HEAD_DIM] for h in heads], axis=0)
        sink = jnp.concatenate([jnp.full((blk, 1), sink_ref[h] * LOG2E, F32) for h in heads], axis=0)
        k = kk[:, hk * D_HEAD_DIM:(hk + 1) * D_HEAD_DIM]
        v = vv[:, hk * D_HEAD_DIM:(hk + 1) * D_HEAD_DIM]
        v_ones = jnp.concatenate([v, jnp.ones_like(v)], axis=1)
        s = lax.dot_general(q, k, (((1,), (1,)), ((), ())), preferred_element_type=F32) + bias
        m = jnp.maximum(_row_max(s), sink)
        e = jnp.exp2((s - m).astype(v.dtype))
        pv = jnp.dot(e, v_ones, preferred_element_type=F32)
        o = pv[:, :D_HEAD_DIM] / (pv[:, D_HEAD_DIM:] + jnp.exp2(sink - m))
        for n_h, h in enumerate(heads):
            o_ref[:, h * D_HEAD_DIM:(h + 1) * D_HEAD_DIM] = o[n_h * blk:(n_h + 1) * blk, :].astype(o_ref.dtype)


def _win_attn(sinks, q, k, v, *, n_batch, lat, ctx, blk=128):
    t, nq = q.shape
    bps = lat // blk
    lat_blocks = n_batch * bps
    ctx_bps = ctx // blk
    ctx_blk0 = n_batch * lat // ctx

    def sample(j):
        return jnp.where(j < lat_blocks, j // bps, (j - lat_blocks) // ctx_bps)

    def neighbour(j, off):
        b = sample(j)
        n = jnp.where(j < lat_blocks, j - b * bps, 0)
        return b * bps + jnp.clip(n + off, 0, bps - 1)

    kv_specs = [pl.BlockSpec((ctx, LANES), lambda j: (ctx_blk0 + sample(j), 0)),
                pl.BlockSpec((blk, LANES), lambda j: (neighbour(j, -1), 0)),
                pl.BlockSpec((blk, LANES), lambda j: (neighbour(j, 0), 0)),
                pl.BlockSpec((blk, LANES), lambda j: (neighbour(j, 1), 0))]
    return pl.pallas_call(
        functools.partial(_win_kernel, lat_blocks=lat_blocks, blocks_per_sample=bps),
        grid=(t // blk,),
        in_specs=[pl.BlockSpec(memory_space=pltpu.SMEM),
                  pl.BlockSpec((blk, nq), lambda j: (j, 0))] + kv_specs + kv_specs,
        out_specs=pl.BlockSpec((blk, nq), lambda j: (j, 0)),
        out_shape=jax.ShapeDtypeStruct((t, nq), BF16),
        compiler_params=_cparams(1),
        name="win_attn",
    )(sinks, q, k, k, k, k, v, v, v, v)


def _rope_tables(n_batch, lat, ctx, dim, tiled):
    quarter = dim // 4
    pos = jnp.arange(lat, dtype=jnp.int32)
    rows, cols = pos // GRID_W, pos % GRID_W
    inv_freq = ROPE_THETA ** (-jnp.arange(quarter, dtype=F32) / quarter)
    ang_r = rows.astype(F32)[:, None] * inv_freq
    ang_c = cols.astype(F32)[:, None] * inv_freq
    ang = jnp.concatenate([ang_r, ang_r, ang_c, ang_c], axis=-1)
    cos, sin = jnp.cos(ang), jnp.sin(ang)
    first = (jnp.arange(dim) % (2 * quarter)) < quarter
    out = []
    for tab, fill in ((cos, 1.0), (jnp.where(first, -sin, 0.0), 0.0), (jnp.where(first, 0.0, sin), 0.0)):
        if tiled:
            tab = jnp.tile(tab, (1, LANES // dim))
        else:
            tab = jnp.concatenate([tab, jnp.full((lat, LANES - dim), fill, F32)], axis=1)
        out.append(jnp.concatenate([jnp.tile(tab, (n_batch, 1)), jnp.full((n_batch * ctx, LANES), fill, F32)],
                                   axis=0))
    return tuple(out)


def _pick_tf(f, cap):
    return max(tf for tf in range(LANES, cap + 1, LANES) if f % tf == 0)


def kernel(x, c, ctx, c_ctx, ada_w, ada_b, norm_mix, norm_ffn, final_norm, a_wqkv, a_q_norm, a_k_norm, a_wo, b_w_down, b_q_lora_norm, b_kv_lora_norm, b_w_uq, b_w_ukv, b_wo, c_w_in, c_ln_g, c_ln_b, c_w_spatial, c_b_spatial, c_w_out, d_wqkv, d_sinks, d_wo, ffn_w13, ffn_w2, moe_router, moe_w13, moe_w2):
    n_batch, lat, d = x.shape
    n_ctx = ctx.shape[1]
    depth = ada_w.shape[0]
    flash_rows = min(FLASH_ROWS, lat)
    assert n_ctx == SEG and lat % flash_rows == 0 and lat % 512 == 0 and n_batch < MOD_ROWS and depth == 4
    ctx_segs = n_batch * n_ctx // SEG
    segs_per_tile = max(k for k in (4, 2, 1) if ctx_segs % k == 0)
    tm = segs_per_tile * SEG
    geom = dict(segs_per_tile=segs_per_tile, lat_segs=n_batch * lat // SEG, segs_per_sample=lat // SEG,
                n_batch=n_batch)
    half_segs = min(2, segs_per_tile)
    geom_half, tm_half = dict(geom, segs_per_tile=half_segs), half_segs * SEG

    cvec = jnp.concatenate([c, c_ctx[None, :], jnp.zeros((MOD_ROWS - n_batch - 1, d), F32)], axis=0)
    mod = _ada_mod(cvec, ada_w, ada_b)
    h = (x.reshape(n_batch * lat, d), ctx.reshape(n_batch * n_ctx, d))

    tabs_a = _rope_tables(n_batch, lat, n_ctx, A_HEAD_DIM, True)
    tabs_d = _rope_tables(n_batch, lat, n_ctx, D_HEAD_DIM, True)
    tabs_b = _rope_tables(n_batch, lat, n_ctx, B_ROPE_DIM, False)
    row = lambda v: v.reshape(1, -1)
    dense_tf = _pick_tf(ffn_w13.shape[-1] // 2, 256)
    moe_tf = _pick_tf(moe_w13.shape[-1] // 2, 512)
    attn = functools.partial(_flash, n_batch=n_batch, lat=lat, ctx=n_ctx)

    qkv = _proj_a(*h, mod[0], row(norm_mix[0]), a_wqkv[0].astype(BF16), row(a_q_norm[0]), row(a_k_norm[0]),
                  tabs_a, geom, tm)
    grp = A_HEADS // A_KV_HEADS
    o = attn(qkv, qkv, qkv, n_groups=A_KV_HEADS, heads=grp, dq=A_HEAD_DIM, dv=A_HEAD_DIM,
             k_col=lambda g: A_HEADS + g, v_col=lambda g: A_HEADS + A_KV_HEADS + g, tq=flash_rows // grp)
    h = _mix_ffn(o, a_wo[0].astype(BF16), h, mod[0], row(norm_ffn[0]), ffn_w13[0].astype(BF16),
                 ffn_w2[0].astype(BF16), geom_half, tm_half, dense_tf)

    wd = b_w_down[0]
    wd = jnp.concatenate([wd, jnp.zeros((d, LANES - B_ROPE_DIM), F32)], axis=1).astype(BF16)
    wq = b_w_uq[0].reshape(B_Q_LORA, B_HEADS, B_NOPE_DIM + B_ROPE_DIM)
    wq = jnp.concatenate([wq, jnp.zeros((B_Q_LORA, B_HEADS, MXU_N - B_NOPE_DIM - B_ROPE_DIM), F32)], axis=2)
    wq = wq.reshape(B_Q_LORA, B_HEADS * MXU_N).astype(BF16)
    wkv = b_w_ukv[0].reshape(B_KV_LORA, B_HEADS, B_NOPE_DIM + B_V_DIM)
    wk = wkv[:, :, :B_NOPE_DIM].reshape(B_KV_LORA, B_HEADS * B_NOPE_DIM).astype(BF16)
    wv = wkv[:, :, B_NOPE_DIM:].reshape(B_KV_LORA, B_HEADS * B_V_DIM).astype(BF16)
    qc, kc, vb = _proj_b(h, mod[1], row(norm_mix[1]), wd, row(b_q_lora_norm[0]), row(b_kv_lora_norm[0]),
                         wq, wk, wv, tabs_b, geom, tm)
    o = attn(qc, kc, vb, n_groups=B_HEADS, heads=1, dq=MXU_N, dv=B_V_DIM,
             k_col=lambda g: g, v_col=lambda g: g, tq=flash_rows)
    h = _mix_moe(o, b_wo[0].astype(BF16), h, mod[1], row(norm_ffn[1]), moe_router[0], moe_w13[0].astype(BF16),
                 moe_w2[0].astype(BF16), geom, tm, moe_tf)

    bs = jnp.repeat(c_b_spatial[0].T, c_w_out.shape[1] // C_GROUPS, axis=1)
    gated = _sgu(h, mod[2], row(norm_mix[2]), c_w_in[0].astype(BF16), row(c_ln_g[0]), row(c_ln_b[0]),
                 c_w_spatial[0].astype(BF16), bs, geom_half, tm_half)
    h = _mix_ffn(gated, c_w_out[0].astype(BF16), h, mod[2], row(norm_ffn[2]), ffn_w13[1].astype(BF16),
                 ffn_w2[1].astype(BF16), geom_half, tm_half, dense_tf)

    q3, k3, v3 = _proj_d(h, mod[3], row(norm_mix[3]), d_wqkv[0].astype(BF16), tabs_d, geom, tm)
    o = _win_attn(d_sinks[0], q3, k3, v3, n_batch=n_batch, lat=lat, ctx=n_ctx)
    out = _mix_moe(o, d_wo[0].astype(BF16), h, mod[3], row(norm_ffn[3]), moe_router[1], moe_w13[1].astype(BF16),
                   moe_w2[1].astype(BF16), geom, tm, moe_tf, final=(row(final_norm), n_batch * lat))
    return out.reshape(n_batch, lat, d)
```

```python
import functools
import math

import jax
import jax.numpy as jnp
from jax import lax
from jax.experimental import pallas as pl
from jax.experimental.pallas import tpu as pltpu

F32 = jnp.float32
BF16 = jnp.bfloat16

EPS = 1e-6
ROPE_THETA = 10000.0
GRID_W = 64
MASK_VALUE = -1e30
LOG2E = math.log2(math.e)

A_HEADS, A_KV_HEADS, A_HEAD_DIM = 8, 2, 128
B_HEADS, B_Q_LORA, B_KV_LORA, B_NOPE_DIM, B_ROPE_DIM, B_V_DIM = 8, 384, 256, 128, 64, 128
C_GROUPS, C_CHUNK = 8, 128
D_HEADS, D_KV_HEADS, D_HEAD_DIM, D_WINDOW = 16, 2, 64, 128
TOP_K = 2

SEG = 256
LANES = 128
MXU_N = 256
VMEM_LIMIT = 56 * 1024 * 1024

MOD_ROWS = 8
INFO_COLS = 8


def _cparams(n_axes):
    return pltpu.CompilerParams(dimension_semantics=("arbitrary",) * n_axes,
                                vmem_limit_bytes=VMEM_LIMIT)


def _full(a):
    return pl.BlockSpec(a.shape, lambda *_: (0,) * a.ndim)


def _mod_row(tile, s, segs_per_tile, lat_segs, segs_per_sample, n_batch):
    g = tile * segs_per_tile + s
    return jnp.where(g < lat_segs, lax.div(g, jnp.int32(segs_per_sample)), n_batch)


def _mod_vec(mod_ref, row, k, d):
    return mod_ref[pl.ds(row, 1), k * d:(k + 1) * d]


def _row_sum(x):
    return jnp.sum(x, axis=-1, keepdims=True)


def _row_max(x):
    return jnp.max(x, axis=-1, keepdims=True)


def _rms(x):
    return x * lax.rsqrt(jnp.mean(x * x, axis=-1, keepdims=True) + EPS)


def _modulate_tile(x_ref, xn_ref, mod_ref, g_ref, tile, k_shift, geom):
    d = x_ref.shape[-1]
    g = g_ref[...]
    for s in range(geom["segs_per_tile"]):
        row = _mod_row(tile, s, **geom)
        shift = _mod_vec(mod_ref, row, k_shift, d)
        scale = _mod_vec(mod_ref, row, k_shift + 1, d)
        x = x_ref[s * SEG:(s + 1) * SEG, :]
        y = _rms(x) * g * (1.0 + scale) + shift
        xn_ref[s * SEG:(s + 1) * SEG, :] = y.astype(xn_ref.dtype)


def _rope(t, cos, sin_a, sin_b, shift):
    return (t * cos + pltpu.roll(t, LANES - shift, axis=1) * sin_a
            + pltpu.roll(t, shift, axis=1) * sin_b)


def _ada_kernel(c_ref, w_ref, b_ref, o_ref):
    c = c_ref[...]
    a = c * jax.nn.sigmoid(c)
    o_ref[0] = jnp.dot(a, w_ref[0], preferred_element_type=F32,
                       precision=lax.Precision.HIGHEST) + b_ref[0]


def _ada_mod(cvec, ada_w, ada_b):
    depth, d, m = ada_w.shape
    tn = m // 4
    return pl.pallas_call(
        _ada_kernel,
        grid=(depth, m // tn),
        in_specs=[pl.BlockSpec((MOD_ROWS, d), lambda l, j: (0, 0)),
                  pl.BlockSpec((1, d, tn), lambda l, j: (l, 0, j)),
                  pl.BlockSpec((1, 1, tn), lambda l, j: (l, 0, j))],
        out_specs=pl.BlockSpec((1, MOD_ROWS, tn), lambda l, j: (l, 0, j)),
        out_shape=jax.ShapeDtypeStruct((depth, MOD_ROWS, m), F32),
        compiler_params=_cparams(2),
        name="ada_mod",
    )(cvec, ada_w, ada_b.reshape(depth, 1, m))


def _stream_tile(xl_ref, xc_ref, buf_ref, i, n_lat_tiles):
    @pl.when(i < n_lat_tiles)
    def _():
        buf_ref[...] = xl_ref[...]

    @pl.when(i >= n_lat_tiles)
    def _():
        buf_ref[...] = xc_ref[...]


def _stream_specs(tm, d, n_lat_tiles):
    return [pl.BlockSpec((tm, d), lambda i: (jnp.minimum(i, n_lat_tiles - 1), 0)),
            pl.BlockSpec((tm, d), lambda i: (jnp.maximum(i - n_lat_tiles, 0), 0))]


def _proj_a_kernel(xl_ref, xc_ref, mod_ref, g_ref, w_ref, qn_ref, kn_ref, cos_ref, sa_ref, sb_ref, o_ref,
                   x_ref, xn_ref, *, geom, scale, n_lat_tiles):
    i = pl.program_id(0)
    _stream_tile(xl_ref, xc_ref, x_ref, i, n_lat_tiles)
    _modulate_tile(x_ref, xn_ref, mod_ref, g_ref, i, 0, geom)
    xn = xn_ref[...]
    cos, sa, sb = cos_ref[...], sa_ref[...], sb_ref[...]
    qn, kn = qn_ref[...], kn_ref[...]
    n_q, n_k = A_HEADS, A_KV_HEADS
    n_heads = n_q + 2 * n_k
    for pair in range(n_heads // 2):
        y2 = jnp.dot(xn, w_ref[:, pair * MXU_N:(pair + 1) * MXU_N], preferred_element_type=F32)
        for half in range(2):
            h = 2 * pair + half
            y = y2[:, half * LANES:(half + 1) * LANES]
            if h < n_q:
                y = _rope(_rms(y) * qn, cos, sa, sb, A_HEAD_DIM // 4) * scale
            elif h < n_q + n_k:
                y = _rope(_rms(y) * kn, cos, sa, sb, A_HEAD_DIM // 4)
            o_ref[:, h * LANES:(h + 1) * LANES] = y.astype(o_ref.dtype)


def _proj_a(xl, xc, mod, g, w, qn, kn, tabs, geom, tm):
    d = xl.shape[1]
    t = xl.shape[0] + xc.shape[0]
    n = w.shape[1]
    n_lat_tiles = xl.shape[0] // tm
    rows = lambda width: pl.BlockSpec((tm, width), lambda i: (i, 0))
    return pl.pallas_call(
        functools.partial(_proj_a_kernel, geom=geom, scale=A_HEAD_DIM ** -0.5 * LOG2E, n_lat_tiles=n_lat_tiles),
        grid=(t // tm,),
        in_specs=_stream_specs(tm, d, n_lat_tiles) + [_full(mod), _full(g), _full(w), _full(qn), _full(kn)]
        + _table_specs(geom, tm),
        out_specs=rows(n),
        out_shape=jax.ShapeDtypeStruct((t, n), BF16),
        scratch_shapes=[pltpu.VMEM((tm, d), F32), pltpu.VMEM((tm, d), BF16)],
        compiler_params=_cparams(1),
        name="proj_a",
    )(xl, xc, mod, g, w, qn, kn, *tabs)


FLASH_ROWS = 2048
FLASH_UNROLL = 16


def _flash_kernel(*refs, heads, dq, dv, tk, n_lat_chunks):
    if n_lat_chunks:
        _, q_ref, kc_ref, vc_ref, kl_ref, vl_ref, o_ref = refs
    else:
        _, q_ref, kc_ref, vc_ref, o_ref = refs
    tq = q_ref.shape[0]
    if heads > 1:
        q = jnp.concatenate([q_ref[:, g * dq:(g + 1) * dq] for g in range(heads)], axis=0)
    else:
        q = q_ref[...]
    rows = heads * tq

    def scores(k):
        return lax.dot_general(q, k, (((1,), (1,)), ((), ())), preferred_element_type=F32)

    def update(s, v, m, acc):
        m_new = jnp.maximum(m, _row_max(s))
        alpha = jnp.exp2(m - m_new)
        p = jnp.exp2((s - m_new).astype(v.dtype))
        v_ones = jnp.concatenate([v, jnp.ones_like(v)], axis=1)
        acc = alpha * acc + jnp.dot(p, v_ones, preferred_element_type=F32)
        return m_new, acc

    m0 = jnp.full((rows, 1), MASK_VALUE, F32)
    acc0 = jnp.zeros((rows, 2 * dv), F32)
    m, acc = update(scores(kc_ref[...]), vc_ref[...], m0, acc0)
    if n_lat_chunks:
        def body(c, carry):
            off = pl.multiple_of(c * tk, tk)
            return update(scores(kl_ref[pl.ds(off, tk), :]), vl_ref[pl.ds(off, tk), :], *carry)

        m, acc = lax.fori_loop(0, n_lat_chunks, body, (m, acc), unroll=FLASH_UNROLL)
    o = acc[:, :dv] / acc[:, dv:]
    for g in range(heads):
        o_ref[:, g * dv:(g + 1) * dv] = o[g * tq:(g + 1) * tq, :].astype(o_ref.dtype)


def _flash(q, k, v, *, n_batch, lat, ctx, n_groups, heads, dq, dv, k_col, v_col, tq, tk=512):
    t = q.shape[0]
    nq = lat // tq
    ctx_blk0 = n_batch * lat // ctx
    kern = functools.partial(_flash_kernel, heads=heads, dq=dq, dv=dv, tk=tk)
    out_shape = jax.ShapeDtypeStruct((t, n_groups * heads * dv), BF16)
    o = pl.pallas_call(
        functools.partial(kern, n_lat_chunks=lat // tk),
        grid=(n_batch, n_groups, nq),
        in_specs=[pl.BlockSpec(memory_space=pl.ANY),
                  pl.BlockSpec((tq, heads * dq), lambda b, g, i: (b * nq + i, g)),
                  pl.BlockSpec((ctx, dq), lambda b, g, i: (ctx_blk0 + b, k_col(g))),
                  pl.BlockSpec((ctx, dv), lambda b, g, i: (ctx_blk0 + b, v_col(g))),
                  pl.BlockSpec((lat, dq), lambda b, g, i: (b, k_col(g))),
                  pl.BlockSpec((lat, dv), lambda b, g, i: (b, v_col(g)))],
        out_specs=pl.BlockSpec((tq, heads * dv), lambda b, g, i: (b * nq + i, g)),
        out_shape=out_shape,
        input_output_aliases={0: 0},
        compiler_params=_cparams(3),
        name="flash_lat",
    )(jnp.zeros(out_shape.shape, out_shape.dtype), q, k, v, k, v)
    return pl.pallas_call(
        functools.partial(kern, n_lat_chunks=0),
        grid=(n_batch, n_groups),
        in_specs=[pl.BlockSpec(memory_space=pl.ANY),
                  pl.BlockSpec((ctx, heads * dq), lambda b, g: (ctx_blk0 + b, g)),
                  pl.BlockSpec((ctx, dq), lambda b, g: (ctx_blk0 + b, k_col(g))),
                  pl.BlockSpec((ctx, dv), lambda b, g: (ctx_blk0 + b, v_col(g)))],
        out_specs=pl.BlockSpec((ctx, heads * dv), lambda b, g: (ctx_blk0 + b, g)),
        out_shape=out_shape,
        input_output_aliases={0: 0},
        compiler_params=_cparams(2),
        name="flash_ctx",
    )(o, q, k, v)


def _swiglu(xn, w13_ref, w2_ref, tf):
    f_total = w2_ref.shape[0]
    acc = None
    for f in range(f_total // tf):
        a = jnp.dot(xn, w13_ref[:, f * tf:(f + 1) * tf], preferred_element_type=F32)
        b = jnp.dot(xn, w13_ref[:, f_total + f * tf:f_total + (f + 1) * tf], preferred_element_type=F32)
        y = (a * jax.nn.sigmoid(a) * b).astype(BF16)
        part = jnp.dot(y, w2_ref[f * tf:(f + 1) * tf, :], preferred_element_type=F32)
        acc = part if acc is None else acc + part
    return acc


def _mix_ffn_kernel(a_ref, wo_ref, *refs, geom, tf, n_lat_tiles):
    i = pl.program_id(0)
    if n_lat_tiles is None:
        h_ref, mod_ref, g_ref, w13_ref, w2_ref, o_ref, h1_ref, xn_ref = refs
        h1_ref[...] = h_ref[...]
    else:
        hl_ref, hc_ref, mod_ref, g_ref, w13_ref, w2_ref, o_ref, h1_ref, xn_ref = refs
        _stream_tile(hl_ref, hc_ref, h1_ref, i, n_lat_tiles)
    d = h1_ref.shape[-1]
    segs = [slice(s * SEG, (s + 1) * SEG) for s in range(geom["segs_per_tile"])]
    rows = [_mod_row(i, s, **geom) for s in range(geom["segs_per_tile"])]
    for sl, row in zip(segs, rows):
        y = jnp.dot(a_ref[sl, :], wo_ref[...], preferred_element_type=F32)
        h1_ref[sl, :] += _mod_vec(mod_ref, row, 2, d) * y
    _modulate_tile(h1_ref, xn_ref, mod_ref, g_ref, i, 3, geom)
    acc = _swiglu(xn_ref[...], w13_ref, w2_ref, tf)
    for sl, row in zip(segs, rows):
        o_ref[sl, :] = h1_ref[sl, :] + _mod_vec(mod_ref, row, 5, d) * acc[sl, :]


def _resident(a):
    return pl.BlockSpec(a.shape, lambda *_: (0,) * a.ndim, pipeline_mode=pl.Buffered(1))


def _mix_ffn(a, wo, h, mod, g, w13, w2, geom, tm, tf):
    t, ka = a.shape
    two = isinstance(h, tuple)
    d = (h[0] if two else h).shape[1]
    n_lat_tiles = h[0].shape[0] // tm if two else None
    h_specs = _stream_specs(tm, d, n_lat_tiles) if two else [pl.BlockSpec((tm, d), lambda i: (i, 0))]
    return pl.pallas_call(
        functools.partial(_mix_ffn_kernel, geom=geom, tf=tf, n_lat_tiles=n_lat_tiles),
        grid=(t // tm,),
        in_specs=[pl.BlockSpec((tm, ka), lambda i: (i, 0)), _resident(wo)] + h_specs
        + [_full(mod), _full(g), _resident(w13), _resident(w2)],
        out_specs=pl.BlockSpec((tm, d), lambda i: (i, 0)),
        out_shape=jax.ShapeDtypeStruct((t, d), F32),
        scratch_shapes=[pltpu.VMEM((tm, d), F32), pltpu.VMEM((tm, d), BF16)],
        compiler_params=_cparams(1),
        name="mix_ffn",
    )(a, wo, *(h if two else (h,)), mod, g, w13, w2)


def _router_kernel(a_ref, wo_ref, h_ref, mod_ref, g_ref, wr_ref, x_ref, info_ref, cnt_ref, xn_ref, tri_ref,
                   run_ref, *, geom):
    i = pl.program_id(0)
    tm, d = x_ref.shape

    @pl.when(i == 0)
    def _():
        r = lax.broadcasted_iota(jnp.int32, (tm, tm), 0)
        c = lax.broadcasted_iota(jnp.int32, (tm, tm), 1)
        tri_ref[...] = jnp.where(c < r, 1.0, 0.0).astype(tri_ref.dtype)
        run_ref[...] = jnp.zeros_like(run_ref)

    for s in range(geom["segs_per_tile"]):
        gate = _mod_vec(mod_ref, _mod_row(i, s, **geom), 2, d)
        sl = slice(s * SEG, (s + 1) * SEG)
        x_ref[sl, :] = h_ref[sl, :] + gate * jnp.dot(a_ref[sl, :], wo_ref[...], preferred_element_type=F32)
    _modulate_tile(x_ref, xn_ref, mod_ref, g_ref, i, 3, geom)
    logits = jnp.dot(xn_ref[...], wr_ref[...], preferred_element_type=F32, precision=lax.Precision.HIGHEST)
    n_e = float(logits.shape[1])
    lane = lax.broadcasted_iota(jnp.int32, logits.shape, 1).astype(F32)
    m1 = jnp.max(logits, axis=1, keepdims=True)
    i1 = jnp.min(jnp.where(logits == m1, lane, n_e), axis=1, keepdims=True)
    rest = jnp.where(lane == i1, -jnp.inf, logits)
    m2 = jnp.max(rest, axis=1, keepdims=True)
    i2 = jnp.min(jnp.where(rest == m2, lane, n_e), axis=1, keepdims=True)
    e2 = jnp.exp(m2 - m1)
    w1 = 1.0 / (1.0 + e2)
    w2 = e2 / (1.0 + e2)
    onehot = jnp.where((lane == i1) | (lane == i2), 1.0, 0.0)
    before = jnp.dot(tri_ref[...], onehot.astype(tri_ref.dtype), preferred_element_type=F32) + run_ref[...]
    r1 = jnp.sum(jnp.where(lane == i1, before, 0.0), axis=1, keepdims=True)
    r2 = jnp.sum(jnp.where(lane == i2, before, 0.0), axis=1, keepdims=True)
    run_ref[...] += jnp.sum(onehot, axis=0, keepdims=True)
    cnt_ref[...] = run_ref[...]
    vals = (i1, i2, w1, w2, r1, r2)
    info = jnp.zeros_like(logits)
    for k, val in enumerate(vals):
        info = jnp.where(lane == float(k), val, info)
    info_ref[...] = info


def _router(a, wo, h, mod, g, wr, geom, tm):
    t, ka = a.shape
    d = h.shape[1]
    n_e = wr.shape[1]
    assert n_e == INFO_COLS
    rows = lambda width: pl.BlockSpec((tm, width), lambda i: (i, 0))
    return pl.pallas_call(
        functools.partial(_router_kernel, geom=geom),
        grid=(t // tm,),
        in_specs=[rows(ka), _full(wo), rows(d), _full(mod), _full(g), _full(wr)],
        out_specs=[rows(d), rows(INFO_COLS), pl.BlockSpec((1, n_e), lambda i: (0, 0))],
        out_shape=[jax.ShapeDtypeStruct((t, d), F32), jax.ShapeDtypeStruct((t, INFO_COLS), F32),
                   jax.ShapeDtypeStruct((1, n_e), F32)],
        scratch_shapes=[pltpu.VMEM((tm, d), F32), pltpu.VMEM((tm, tm), BF16), pltpu.VMEM((1, n_e), F32)],
        compiler_params=_cparams(1),
        name="router",
    )(a, wo, h, mod, g, wr)


EXPERT_TILE_ROWS = 512
DMA_ISSUE_UNROLL = 8


def _row_copy(src_ref, src_row, dst_ref, dst_row, sem):
    return pltpu.make_async_copy(src_ref.at[pl.ds(src_row, 1)], dst_ref.at[pl.ds(dst_row, 1)], sem)


def _dispatch_kernel(slots_ref, ends_ref, x_ref, mod_ref, g_ref, xs_ref, xn_ref, sem, *, geom, n_tiles, rows):
    i = pl.program_id(0)
    tm = x_ref.shape[0]
    n_e = ends_ref.shape[0]

    @pl.when(i == 0)
    def _():
        xn_ref[0:rows, :] = jnp.zeros((rows, xn_ref.shape[1]), xn_ref.dtype)

        def zero_tile(row0):
            cp = pltpu.make_async_copy(xn_ref.at[pl.ds(0, rows)], xs_ref.at[pl.ds(row0, rows)], sem)
            cp.start()
            cp.wait()

        for e in range(n_e):
            start = ends_ref[e - 1] if e else 0
            last_tile = pl.multiple_of(ends_ref[e] - rows, rows)
            pl.when(ends_ref[e] > start)(functools.partial(zero_tile, last_tile))
        for j in range(n_tiles - n_e, n_tiles):
            pl.when(j * rows >= ends_ref[n_e - 1])(functools.partial(zero_tile, j * rows))

    _modulate_tile(x_ref, xn_ref, mod_ref, g_ref, i, 3, geom)

    def body(r, carry):
        for k in range(TOP_K):
            _row_copy(xn_ref, r, xs_ref, slots_ref[TOP_K * r + k], sem).start(priority=k)
        return carry

    lax.fori_loop(0, tm, body, 0, unroll=DMA_ISSUE_UNROLL)
    for k in range(TOP_K):
        pltpu.make_async_copy(xn_ref, xs_ref.at[pl.ds(0, tm)], sem).wait()


def _dispatch(slots, ends, x, mod, g, n_tiles, rows, geom, tm):
    t, d = x.shape
    assert rows <= tm
    return pl.pallas_call(
        functools.partial(_dispatch_kernel, geom=geom, n_tiles=n_tiles, rows=rows),
        grid=(t // tm,),
        in_specs=[pl.BlockSpec((TOP_K * tm,), lambda i: (i,), memory_space=pltpu.SMEM),
                  pl.BlockSpec(memory_space=pltpu.SMEM),
                  pl.BlockSpec((tm, d), lambda i: (i, 0)), _full(mod), _full(g)],
        out_specs=pl.BlockSpec(memory_space=pl.ANY),
        out_shape=jax.ShapeDtypeStruct((n_tiles * rows, d), F32),
        scratch_shapes=[pltpu.VMEM((tm, d), F32), pltpu.SemaphoreType.DMA(())],
        compiler_params=_cparams(1),
        name="moe_dispatch",
    )(slots, ends, x, mod, g)


def _experts_kernel(tile_expert_ref, n_used_ref, xs_ref, w13_ref, w2_ref, y_ref, *, tf):
    del tile_expert_ref
    i = pl.program_id(0)

    @pl.when(i < n_used_ref[0])
    def _():
        y_ref[...] = _swiglu(xs_ref[...].astype(BF16), w13_ref, w2_ref, tf)

    @pl.when(i >= n_used_ref[0])
    def _():
        y_ref[...] = jnp.zeros_like(y_ref)


def _experts(tile_expert, n_used, xs, w13, w2, rows, tf):
    p, d = xs.shape
    _, f_total, _ = w2.shape
    once = pl.Buffered(1)
    grid_spec = pltpu.PrefetchScalarGridSpec(
        num_scalar_prefetch=2,
        grid=(p // rows,),
        in_specs=[pl.BlockSpec((rows, d), lambda i, te, nu: (i, 0)),
                  pl.BlockSpec((None, d, 2 * f_total), lambda i, te, nu: (te[i], 0, 0), pipeline_mode=once),
                  pl.BlockSpec((None, f_total, d), lambda i, te, nu: (te[i], 0, 0), pipeline_mode=once)],
        out_specs=pl.BlockSpec((rows, d), lambda i, te, nu: (i, 0)),
    )
    return pl.pallas_call(
        functools.partial(_experts_kernel, tf=tf),
        grid_spec=grid_spec,
        out_shape=jax.ShapeDtypeStruct((p, d), F32),
        compiler_params=_cparams(1),
        name="moe_experts",
    )(tile_expert, n_used, xs, w13, w2)


def _combine_kernel(slots_ref, x_ref, info_ref, mod_ref, y_ref, *refs, geom):
    fg_ref = refs[0] if len(refs) == 4 else None
    o_ref, yb_ref, sem = refs[-3:]
    i = pl.program_id(0)
    tm, d = x_ref.shape

    def body(r, carry):
        for k in range(TOP_K):
            _row_copy(y_ref, slots_ref[TOP_K * r + k], yb_ref.at[k], r, sem).start(priority=k)
        return carry

    lax.fori_loop(0, tm, body, 0, unroll=DMA_ISSUE_UNROLL)
    for k in range(TOP_K):
        pltpu.make_async_copy(y_ref.at[pl.ds(0, tm)], yb_ref.at[k], sem).wait()

    lane = lax.broadcasted_iota(jnp.int32, (SEG, INFO_COLS), 1)
    for s in range(geom["segs_per_tile"]):
        row = _mod_row(i, s, **geom)
        gate2 = _mod_vec(mod_ref, row, 5, d)
        sl = slice(s * SEG, (s + 1) * SEG)
        info = info_ref[sl, :]
        w1 = jnp.sum(jnp.where(lane == 2, info, 0.0), axis=1, keepdims=True)
        w2 = jnp.sum(jnp.where(lane == 3, info, 0.0), axis=1, keepdims=True)
        out = x_ref[sl, :] + gate2 * (w1 * yb_ref[0, sl, :] + w2 * yb_ref[1, sl, :])
        o_ref[sl, :] = out if fg_ref is None else _rms(out) * fg_ref[...]


def _combine(slots, x, info, mod, y, geom, tm, final=None):
    t, d = x.shape
    n_rows = final[1] if final else t
    extra = [final[0]] if final else []
    return pl.pallas_call(
        functools.partial(_combine_kernel, geom=geom),
        grid=(n_rows // tm,),
        in_specs=[pl.BlockSpec((TOP_K * tm,), lambda i: (i,), memory_space=pltpu.SMEM),
                  pl.BlockSpec((tm, d), lambda i: (i, 0)),
                  pl.BlockSpec((tm, INFO_COLS), lambda i: (i, 0)), _full(mod),
                  pl.BlockSpec(memory_space=pl.ANY)] + [_full(a) for a in extra],
        out_specs=pl.BlockSpec((tm, d), lambda i: (i, 0)),
        out_shape=jax.ShapeDtypeStruct((n_rows, d), F32),
        scratch_shapes=[pltpu.VMEM((TOP_K, tm, d), F32), pltpu.SemaphoreType.DMA(())],
        compiler_params=_cparams(1),
        name="moe_combine",
    )(slots, x, info, mod, y, *extra)


def _mix_moe(a, wo, h, mod, g, wr, w13, w2, geom, tm, tf, final=None):
    t, d = a.shape[0], h.shape[1]
    n_e = wr.shape[1]
    x, info, counts = _router(a, wo, h, mod, g, wr, geom, tm)
    rows = EXPERT_TILE_ROWS
    counts = counts[0].astype(jnp.int32)
    padded = (counts + rows - 1) // rows * rows
    ends = jnp.cumsum(padded)
    starts = ends - padded
    n_tiles = TOP_K * t // rows + n_e
    tile_start = jnp.arange(n_tiles, dtype=jnp.int32) * rows
    tile_expert = jnp.minimum(jnp.sum((ends[None, :] <= tile_start[:, None]).astype(jnp.int32), axis=1), n_e - 1)
    n_used = (ends[-1:] // rows).astype(jnp.int32)
    experts = info[:, :TOP_K].astype(jnp.int32)
    ranks = info[:, 4:4 + TOP_K].astype(jnp.int32)
    slots = (starts[experts] + ranks).reshape(-1)
    xs = _dispatch(slots, ends, x, mod, g, n_tiles, rows, geom, tm)
    y = _experts(tile_expert, n_used, xs, w13, w2, rows, tf)
    return _combine(slots, x, info, mod, y, geom, tm, final)


def _proj_b_kernel(x_ref, mod_ref, g_ref, wd_ref, qg_ref, kvg_ref, wq_ref, wk_ref, wv_ref,
                   cos_ref, sa_ref, sb_ref, q_ref, k_ref, v_ref, xn_ref, *, geom, scale):
    i = pl.program_id(0)
    _modulate_tile(x_ref, xn_ref, mod_ref, g_ref, i, 0, geom)
    xn = xn_ref[...]
    cos, sa, sb = cos_ref[...], sa_ref[...], sb_ref[...]
    dq = jnp.dot(xn, wd_ref[:, :B_Q_LORA], preferred_element_type=F32)
    dkv = jnp.dot(xn, wd_ref[:, B_Q_LORA:B_Q_LORA + B_KV_LORA], preferred_element_type=F32)
    kr = jnp.dot(xn, wd_ref[:, B_Q_LORA + B_KV_LORA:], preferred_element_type=F32)
    dqn = (_rms(dq) * qg_ref[...]).astype(BF16)
    dkvn = (_rms(dkv) * kvg_ref[...]).astype(BF16)
    kr = _rope(kr, cos, sa, sb, B_ROPE_DIM // 4).astype(k_ref.dtype)
    for h in range(B_HEADS):
        qh = jnp.dot(dqn, wq_ref[:, h * MXU_N:(h + 1) * MXU_N], preferred_element_type=F32)
        q_ref[:, h * MXU_N:h * MXU_N + LANES] = (qh[:, :LANES] * scale).astype(q_ref.dtype)
        q_ref[:, h * MXU_N + LANES:(h + 1) * MXU_N] = (
            _rope(qh[:, LANES:], cos, sa, sb, B_ROPE_DIM // 4) * scale).astype(q_ref.dtype)
        k_ref[:, h * MXU_N + LANES:(h + 1) * MXU_N] = kr
    for pair in range(B_HEADS // 2):
        sl = slice(pair * MXU_N, (pair + 1) * MXU_N)
        kn = jnp.dot(dkvn, wk_ref[:, sl], preferred_element_type=F32)
        for half in range(2):
            h = 2 * pair + half
            k_ref[:, h * MXU_N:h * MXU_N + LANES] = kn[:, half * LANES:(half + 1) * LANES].astype(k_ref.dtype)
        v_ref[:, sl] = jnp.dot(dkvn, wv_ref[:, sl], preferred_element_type=F32).astype(v_ref.dtype)


def _proj_b(x, mod, g, wd, qg, kvg, wq, wk, wv, tabs, geom, tm):
    t, d = x.shape
    rows = lambda width: pl.BlockSpec((tm, width), lambda i: (i, 0))
    nq, nv = B_HEADS * MXU_N, B_HEADS * B_V_DIM
    return pl.pallas_call(
        functools.partial(_proj_b_kernel, geom=geom, scale=(B_NOPE_DIM + B_ROPE_DIM) ** -0.5 * LOG2E),
        grid=(t // tm,),
        in_specs=[rows(d), _full(mod), _full(g), _full(wd), _full(qg), _full(kvg), _full(wq), _full(wk),
                  _full(wv)] + _table_specs(geom, tm),
        out_specs=[rows(nq), rows(nq), rows(nv)],
        out_shape=[jax.ShapeDtypeStruct((t, nq), BF16), jax.ShapeDtypeStruct((t, nq), BF16),
                   jax.ShapeDtypeStruct((t, nv), BF16)],
        scratch_shapes=[pltpu.VMEM((tm, d), BF16)],
        compiler_params=_cparams(1),
        name="proj_b",
    )(x, mod, g, wd, qg, kvg, wq, wk, wv, *tabs)


def _gelu(x):
    return 0.5 * x * (1.0 + lax.erf(x * (2.0 ** -0.5)))


def _sgu_kernel(x_ref, mod_ref, g_ref, win_ref, lng_ref, lnb_ref, ws_ref, bs_ref, gated_ref,
                xn_ref, u_ref, v_ref, *, geom):
    i = pl.program_id(0)
    tm = x_ref.shape[0]
    cw = u_ref.shape[1]
    _modulate_tile(x_ref, xn_ref, mod_ref, g_ref, i, 0, geom)
    xn = xn_ref[...]
    for j in range(cw // MXU_N):
        sl = slice(j * MXU_N, (j + 1) * MXU_N)
        u_ref[:, sl] = _gelu(jnp.dot(xn, win_ref[:, sl], preferred_element_type=F32))
        v_ref[:, sl] = _gelu(jnp.dot(xn, win_ref[:, cw + j * MXU_N:cw + (j + 1) * MXU_N],
                                     preferred_element_type=F32))
    lng, lnb = lng_ref[...], lnb_ref[...]
    gw = cw // C_GROUPS
    for c in range(tm // C_CHUNK):
        rs = slice(c * C_CHUNK, (c + 1) * C_CHUNK)
        v = v_ref[rs, :]
        mu = _row_sum(v) * (1.0 / cw)
        vc = v - mu
        var = _row_sum(vc * vc) * (1.0 / cw)
        vn = (vc * lax.rsqrt(var + EPS) * lng + lnb).astype(BF16)
        for gi in range(C_GROUPS):
            cs = slice(gi * gw, (gi + 1) * gw)
            mixed = jnp.dot(ws_ref[gi], vn[:, cs], preferred_element_type=F32) + bs_ref[:, cs]
            gated_ref[rs, cs] = (u_ref[rs, cs] * mixed).astype(gated_ref.dtype)


def _sgu(x, mod, g, win, lng, lnb, ws, bs, geom, tm):
    t, d = x.shape
    cw = win.shape[1] // 2
    return pl.pallas_call(
        functools.partial(_sgu_kernel, geom=geom),
        grid=(t // tm,),
        in_specs=[pl.BlockSpec((tm, d), lambda i: (i, 0)), _full(mod), _full(g), _full(win), _full(lng),
                  _full(lnb), _full(ws), _full(bs)],
        out_specs=pl.BlockSpec((tm, cw), lambda i: (i, 0)),
        out_shape=jax.ShapeDtypeStruct((t, cw), BF16),
        scratch_shapes=[pltpu.VMEM((tm, d), BF16), pltpu.VMEM((tm, cw), F32), pltpu.VMEM((tm, cw), F32)],
        compiler_params=_cparams(1),
        name="sgu",
    )(x, mod, g, win, lng, lnb, ws, bs)


def _proj_d_kernel(x_ref, mod_ref, g_ref, w_ref, cos_ref, sa_ref, sb_ref, q_ref, k_ref, v_ref, xn_ref,
                   *, geom, scale):
    i = pl.program_id(0)
    _modulate_tile(x_ref, xn_ref, mod_ref, g_ref, i, 0, geom)
    xn = xn_ref[...]
    cos, sa, sb = cos_ref[...], sa_ref[...], sb_ref[...]
    nq = q_ref.shape[1]
    for pair in range(nq // MXU_N):
        y2 = jnp.dot(xn, w_ref[:, pair * MXU_N:(pair + 1) * MXU_N], preferred_element_type=F32)
        for half in range(2):
            sl = slice(half * LANES, (half + 1) * LANES)
            y = _rope(y2[:, sl], cos, sa, sb, D_HEAD_DIM // 4) * scale
            q_ref[:, pair * MXU_N + half * LANES:pair * MXU_N + (half + 1) * LANES] = y.astype(q_ref.dtype)
    kv = jnp.dot(xn, w_ref[:, nq:nq + MXU_N], preferred_element_type=F32)
    k_ref[...] = _rope(kv[:, :LANES], cos, sa, sb, D_HEAD_DIM // 4).astype(k_ref.dtype)
    v_ref[...] = kv[:, LANES:].astype(v_ref.dtype)


def _proj_d(x, mod, g, w, tabs, geom, tm):
    t, d = x.shape
    nq = D_HEADS * D_HEAD_DIM
    rows = lambda width: pl.BlockSpec((tm, width), lambda i: (i, 0))
    return pl.pallas_call(
        functools.partial(_proj_d_kernel, geom=geom, scale=D_HEAD_DIM ** -0.5 * LOG2E),
        grid=(t // tm,),
        in_specs=[rows(d), _full(mod), _full(g), _full(w)] + _table_specs(geom, tm),
        out_specs=[rows(nq), rows(LANES), rows(LANES)],
        out_shape=[jax.ShapeDtypeStruct((t, nq), BF16), jax.ShapeDtypeStruct((t, LANES), BF16),
                   jax.ShapeDtypeStruct((t, LANES), BF16)],
        scratch_shapes=[pltpu.VMEM((tm, d), BF16)],
        compiler_params=_cparams(1),
        name="proj_d",
    )(x, mod, g, w, *tabs)


WIN_STACK = 4


def _win_kernel(sink_ref, q_ref, kc_ref, km_ref, k0_ref, kp_ref, vc_ref, vm_ref, v0_ref, vp_ref, o_ref,
                *, lat_blocks, blocks_per_sample):
    j = pl.program_id(0)
    blk = q_ref.shape[0]
    ctx = kc_ref.shape[0]
    kk = jnp.concatenate([kc_ref[...], km_ref[...], k0_ref[...], kp_ref[...]], axis=0)
    vv = jnp.concatenate([vc_ref[...], vm_ref[...], v0_ref[...], vp_ref[...]], axis=0)
    n_keys = kk.shape[0]
    n = lax.rem(j, jnp.int32(blocks_per_sample))
    qpos = lax.broadcasted_iota(jnp.int32, (blk, n_keys), 0)
    col = lax.broadcasted_iota(jnp.int32, (blk, n_keys), 1)
    rel = col - ctx - blk - qpos
    kblock = n - 1 + lax.shift_right_arithmetic(col - ctx, int(math.log2(blk)))
    far = jnp.where(j < lat_blocks, 0, 2 * D_WINDOW + 2)
    in_window = (jnp.abs(rel) + far <= D_WINDOW) & (kblock >= 0) & (kblock < blocks_per_sample)
    mask = (col < ctx) | in_window
    group = D_HEADS // D_KV_HEADS
    bias = jnp.concatenate([jnp.where(mask, 0.0, MASK_VALUE)] * WIN_STACK, axis=0)
    for h0 in range(0, D_HEADS, WIN_STACK):
        hk = h0 // group
        heads = range(h0, h0 + WIN_STACK)
        q = jnp.concatenate([q_ref[:, h * D_HEAD_DIM:(h + 1) * D_HEAD_DIM] for h in heads], axis=0)
        sink = jnp.concatenate([jnp.full((blk, 1), sink_ref[h] * LOG2E, F32) for h in heads], axis=0)
        k = kk[:, hk * D_HEAD_DIM:(hk + 1) * D_HEAD_DIM]
        v = vv[:, hk * D_HEAD_DIM:(hk + 1) * D_HEAD_DIM]
        v_ones = jnp.concatenate([v, jnp.ones_like(v)], axis=1)
        s = lax.dot_general(q, k, (((1,), (1,)), ((), ())), preferred_element_type=F32) + bias
        m = jnp.maximum(_row_max(s), sink)
        e = jnp.exp2((s - m).astype(v.dtype))
        pv = jnp.dot(e, v_ones, preferred_element_type=F32)
        o = pv[:, :D_HEAD_DIM] / (pv[:, D_HEAD_DIM:] + jnp.exp2(sink - m))
        for n_h, h in enumerate(heads):
            o_ref[:, h * D_HEAD_DIM:(h + 1) * D_HEAD_DIM] = o[n_h * blk:(n_h + 1) * blk, :].astype(o_ref.dtype)


def _win_attn(sinks, q, k, v, *, n_batch, lat, ctx, need_ctx, blk=128):
    nq = q.shape[1]
    t = q.shape[0] if need_ctx else n_batch * lat
    bps = lat // blk
    lat_blocks = n_batch * bps
    ctx_bps = ctx // blk
    ctx_blk0 = n_batch * lat // ctx

    def sample(j):
        return jnp.where(j < lat_blocks, j // bps, (j - lat_blocks) // ctx_bps)

    def neighbour(j, off):
        b = sample(j)
        n = jnp.where(j < lat_blocks, j - b * bps, 0)
        return b * bps + jnp.clip(n + off, 0, bps - 1)

    kv_specs = [pl.BlockSpec((ctx, LANES), lambda j: (ctx_blk0 + sample(j), 0)),
                pl.BlockSpec((blk, LANES), lambda j: (neighbour(j, -1), 0)),
                pl.BlockSpec((blk, LANES), lambda j: (neighbour(j, 0), 0)),
                pl.BlockSpec((blk, LANES), lambda j: (neighbour(j, 1), 0))]
    return pl.pallas_call(
        functools.partial(_win_kernel, lat_blocks=lat_blocks, blocks_per_sample=bps),
        grid=(t // blk,),
        in_specs=[pl.BlockSpec(memory_space=pltpu.SMEM),
                  pl.BlockSpec((blk, nq), lambda j: (j, 0))] + kv_specs + kv_specs,
        out_specs=pl.BlockSpec((blk, nq), lambda j: (j, 0)),
        out_shape=jax.ShapeDtypeStruct((t, nq), BF16),
        compiler_params=_cparams(1),
        name="win_attn",
    )(sinks, q, k, k, k, k, v, v, v, v)


def _table_specs(geom, tm):
    n_lat_tiles = geom["lat_segs"] // geom["segs_per_tile"]
    per_sample = geom["segs_per_sample"] // geom["segs_per_tile"]
    spec = pl.BlockSpec((tm, LANES), lambda i: (jnp.where(i < n_lat_tiles, i % per_sample, per_sample), 0))
    return [spec] * 3


def _rope_tables(lat, tm, dim, tiled):
    quarter = dim // 4
    pos = jnp.arange(lat, dtype=jnp.int32)
    rows, cols = pos // GRID_W, pos % GRID_W
    inv_freq = ROPE_THETA ** (-jnp.arange(quarter, dtype=F32) / quarter)
    ang_r = rows.astype(F32)[:, None] * inv_freq
    ang_c = cols.astype(F32)[:, None] * inv_freq
    ang = jnp.concatenate([ang_r, ang_r, ang_c, ang_c], axis=-1)
    cos, sin = jnp.cos(ang), jnp.sin(ang)
    first = (jnp.arange(dim) % (2 * quarter)) < quarter
    out = []
    for tab, fill in ((cos, 1.0), (jnp.where(first, -sin, 0.0), 0.0), (jnp.where(first, 0.0, sin), 0.0)):
        if tiled:
            tab = jnp.tile(tab, (1, LANES // dim))
        else:
            tab = jnp.concatenate([tab, jnp.full((lat, LANES - dim), fill, F32)], axis=1)
        out.append(jnp.concatenate([tab, jnp.full((tm, LANES), fill, F32)], axis=0))
    return tuple(out)


def _pick_tf(f, cap):
    return max(tf for tf in range(LANES, cap + 1, LANES) if f % tf == 0)


def kernel(x, c, ctx, c_ctx, ada_w, ada_b, norm_mix, norm_ffn, final_norm, a_wqkv, a_q_norm, a_k_norm, a_wo, b_w_down, b_q_lora_norm, b_kv_lora_norm, b_w_uq, b_w_ukv, b_wo, c_w_in, c_ln_g, c_ln_b, c_w_spatial, c_b_spatial, c_w_out, d_wqkv, d_sinks, d_wo, ffn_w13, ffn_w2, moe_router, moe_w13, moe_w2):
    n_batch, lat, d = x.shape
    n_ctx = ctx.shape[1]
    depth = ada_w.shape[0]
    flash_rows = min(FLASH_ROWS, lat)
    assert n_ctx == SEG and lat % flash_rows == 0 and lat % 512 == 0 and n_batch < MOD_ROWS and depth == 4
    ctx_segs = n_batch * n_ctx // SEG
    segs_per_tile = max(k for k in (4, 2, 1) if ctx_segs % k == 0)
    tm = segs_per_tile * SEG
    geom = dict(segs_per_tile=segs_per_tile, lat_segs=n_batch * lat // SEG, segs_per_sample=lat // SEG,
                n_batch=n_batch)
    half_segs = min(2, segs_per_tile)
    geom_half, tm_half = dict(geom, segs_per_tile=half_segs), half_segs * SEG

    cvec = jnp.concatenate([c, c_ctx[None, :], jnp.zeros((MOD_ROWS - n_batch - 1, d), F32)], axis=0)
    mod = _ada_mod(cvec, ada_w, ada_b)
    h = (x.reshape(n_batch * lat, d), ctx.reshape(n_batch * n_ctx, d))

    assert lat % tm == 0
    tabs_a = _rope_tables(lat, tm, A_HEAD_DIM, True)
    tabs_d = _rope_tables(lat, tm, D_HEAD_DIM, True)
    tabs_b = _rope_tables(lat, tm, B_ROPE_DIM, False)
    row = lambda v: v.reshape(1, -1)
    dense_tf = _pick_tf(ffn_w13.shape[-1] // 2, 256)
    moe_tf = _pick_tf(moe_w13.shape[-1] // 2, 512)
    attn = functools.partial(_flash, n_batch=n_batch, lat=lat, ctx=n_ctx)

    qkv = _proj_a(*h, mod[0], row(norm_mix[0]), a_wqkv[0].astype(BF16), row(a_q_norm[0]), row(a_k_norm[0]),
                  tabs_a, geom, tm)
    grp = A_HEADS // A_KV_HEADS
    o = attn(qkv, qkv, qkv, n_groups=A_KV_HEADS, heads=grp, dq=A_HEAD_DIM, dv=A_HEAD_DIM,
             k_col=lambda g: A_HEADS + g, v_col=lambda g: A_HEADS + A_KV_HEADS + g, tq=flash_rows // grp)
    h = _mix_ffn(o, a_wo[0].astype(BF16), h, mod[0], row(norm_ffn[0]), ffn_w13[0].astype(BF16),
                 ffn_w2[0].astype(BF16), geom_half, tm_half, dense_tf)

    wd = b_w_down[0]
    wd = jnp.concatenate([wd, jnp.zeros((d, LANES - B_ROPE_DIM), F32)], axis=1).astype(BF16)
    wq = b_w_uq[0].reshape(B_Q_LORA, B_HEADS, B_NOPE_DIM + B_ROPE_DIM)
    wq = jnp.concatenate([wq, jnp.zeros((B_Q_LORA, B_HEADS, MXU_N - B_NOPE_DIM - B_ROPE_DIM), F32)], axis=2)
    wq = wq.reshape(B_Q_LORA, B_HEADS * MXU_N).astype(BF16)
    wkv = b_w_ukv[0].reshape(B_KV_LORA, B_HEADS, B_NOPE_DIM + B_V_DIM)
    wk = wkv[:, :, :B_NOPE_DIM].reshape(B_KV_LORA, B_HEADS * B_NOPE_DIM).astype(BF16)
    wv = wkv[:, :, B_NOPE_DIM:].reshape(B_KV_LORA, B_HEADS * B_V_DIM).astype(BF16)
    qc, kc, vb = _proj_b(h, mod[1], row(norm_mix[1]), wd, row(b_q_lora_norm[0]), row(b_kv_lora_norm[0]),
                         wq, wk, wv, tabs_b, geom, tm)
    o = attn(qc, kc, vb, n_groups=B_HEADS, heads=1, dq=MXU_N, dv=B_V_DIM,
             k_col=lambda g: g, v_col=lambda g: g, tq=flash_rows)
    h = _mix_moe(o, b_wo[0].astype(BF16), h, mod[1], row(norm_ffn[1]), moe_router[0], moe_w13[0].astype(BF16),
                 moe_w2[0].astype(BF16), geom, tm, moe_tf)

    bs = jnp.repeat(c_b_spatial[0].T, c_w_out.shape[1] // C_GROUPS, axis=1)
    gated = _sgu(h, mod[2], row(norm_mix[2]), c_w_in[0].astype(BF16), row(c_ln_g[0]), row(c_ln_b[0]),
                 c_w_spatial[0].astype(BF16), bs, geom_half, tm_half)
    h = _mix_ffn(gated, c_w_out[0].astype(BF16), h, mod[2], row(norm_ffn[2]), ffn_w13[1].astype(BF16),
                 ffn_w2[1].astype(BF16), geom_half, tm_half, dense_tf)

    q3, k3, v3 = _proj_d(h, mod[3], row(norm_mix[3]), d_wqkv[0].astype(BF16), tabs_d, geom, tm)
    o = _win_attn(d_sinks[0], q3, k3, v3, n_batch=n_batch, lat=lat, ctx=n_ctx, need_ctx=False)
    out = _mix_moe(o, d_wo[0].astype(BF16), h, mod[3], row(norm_ffn[3]), moe_router[1], moe_w13[1].astype(BF16),
                   moe_w2[1].astype(BF16), geom, tm, moe_tf, final=(row(final_norm), n_batch * lat))
    return out.reshape(n_batch, lat, d)
```
